```python
import math
import jax, jax.numpy as jnp
from jax import lax
import numpy as np

D_MODEL = 1024
BATCH = 16
SEQ = 4096
DEPTH = 2
DEC_BATCH = 8
DEC_SEQ = 8192
PAST_LEN = 128

MLA_HEADS = 8
MLA_NOPE = 64
MLA_ROPE = 32
MLA_V = 64
MLA_QK = MLA_NOPE + MLA_ROPE
Q_LORA = 768
KV_LORA = 256
ROPE_THETA = 10000.0
Q_BLOCK = 128
RWKV_HEAD = 64
RWKV_HEADS = 8
RWKV_DIM = RWKV_HEADS * RWKV_HEAD
DECAY_LORA = 64
AAA_LORA = 64
MV_LORA = 32
GATE_LORA = 128
DECAY_SCALE = 0.606531
GN_EPS = 64e-5
D_FF = 2816
CONV_W = 3
P_DIM = 256
ALPHA = (2 * DEPTH) ** 0.25
BETA = (8 * DEPTH) ** -0.25
LN_EPS = 1e-5
RMS_EPS = 1e-6

RWKV_COLS = 3 * RWKV_DIM + 2 * DECAY_LORA + 2 * AAA_LORA + GATE_LORA
OFF_CKV = Q_LORA
OFF_KR = OFF_CKV + KV_LORA
OFF_RWKV = OFF_KR + MLA_ROPE
OFF_GA = OFF_RWKV + RWKV_COLS
OFF_GB = OFF_GA + D_MODEL
IN_COLS = OFF_GB + D_MODEL

kernel_name = 'hybrid_mla_rwkv7_deepnorm_encoder'


def _layernorm(x, g, b):
    xf = x.astype(jnp.float32)
    mu = jnp.mean(xf, -1, keepdims=True)
    var = jnp.mean(jnp.square(xf - mu), -1, keepdims=True)
    return ((xf - mu) * lax.rsqrt(var + LN_EPS) * g.astype(jnp.float32) + b.astype(jnp.float32)).astype(x.dtype)


def _rmsnorm(x, g):
    xf = x.astype(jnp.float32)
    return (xf * lax.rsqrt(jnp.mean(xf * xf, -1, keepdims=True) + RMS_EPS) * g.astype(jnp.float32)).astype(x.dtype)


def _shift_prev(z):
    return jnp.pad(z[:, :-1], ((0, 0), (1, 0), (0, 0)))


def _shift_next(z):
    return jnp.pad(z[:, 1:], ((0, 0), (0, 1), (0, 0)))


def _rope(x, cos, sin):
    half = x.shape[-1] // 2
    x1, x2 = x[..., :half], x[..., half:]
    return jnp.concatenate([x1 * cos - x2 * sin, x1 * sin + x2 * cos], axis=-1)


def _dwconv3(u, w, b):
    y = lax.conv_general_dilated(u, w[:, None, :].astype(u.dtype), window_strides=(1,), padding=((1, 1),),
                                 dimension_numbers=('NWC', 'WIO', 'NWC'), feature_group_count=u.shape[-1])
    return y + b.astype(u.dtype)


def _mla(c_q, c_kv, k_r, q_norm_g, kv_norm_g, w_uq, w_ukv):
    B, T, _ = c_q.shape
    dt = c_q.dtype
    pos = jnp.arange(T, dtype=jnp.float32)
    inv_freq = ROPE_THETA ** (-jnp.arange(0, MLA_ROPE, 2, dtype=jnp.float32) / MLA_ROPE)
    ang = pos[:, None] * inv_freq[None, :]
    cos, sin = jnp.cos(ang).astype(dt), jnp.sin(ang).astype(dt)
    q = (_rmsnorm(c_q, q_norm_g) @ w_uq).reshape(B, T, MLA_HEADS, MLA_QK)
    q_nope = q[..., :MLA_NOPE]
    q_rope = _rope(q[..., MLA_NOPE:], cos[:, None], sin[:, None])
    kv = (_rmsnorm(c_kv, kv_norm_g) @ w_ukv).reshape(B, T, MLA_HEADS, MLA_NOPE + MLA_V)
    k_nope, v = kv[..., :MLA_NOPE], kv[..., MLA_NOPE:]
    k_rope = _rope(k_r, cos, sin)
    scale = MLA_QK ** -0.5
    nblk = T // Q_BLOCK
    qn = q_nope.reshape(B, nblk, Q_BLOCK, MLA_HEADS, MLA_NOPE).swapaxes(0, 1)
    qr = q_rope.reshape(B, nblk, Q_BLOCK, MLA_HEADS, MLA_ROPE).swapaxes(0, 1)

    def block(args):
        qn_b, qr_b = args
        s = (jnp.einsum('bqhd,bkhd->bhqk', qn_b, k_nope)
             + jnp.einsum('bqhr,bkr->bhqk', qr_b, k_rope)).astype(jnp.float32) * scale
        p = jax.nn.softmax(s, axis=-1).astype(v.dtype)
        return jnp.einsum('bhqk,bkhd->bqhd', p, v)

    o = lax.map(block, (qn, qr))
    return o.swapaxes(0, 1).reshape(B, T, MLA_HEADS * MLA_V)


def _rwkv_scan(r, w, k, v, kk, kka):
    S0 = jnp.zeros(r.shape[1:] + (RWKV_HEAD,), jnp.float32)

    def step(S, inp):
        r_t, w_t, k_t, v_t, kk_t, kka_t = inp
        sa = jnp.einsum('dbhij,dbhj->dbhi', S, kk_t)
        S = S * w_t[..., None, :] - sa[..., :, None] * kka_t[..., None, :] + v_t[..., :, None] * k_t[..., None, :]
        y = jnp.einsum('dbhij,dbhj->dbhi', S, r_t)
        return S, y

    _, y = lax.scan(step, S0, (r, w, k, v, kk, kka))
    return y


def _rwkv_time_mix(z, x, v_first, mu, w0, w_up, a0, a_up, g_up, k_k, k_a, r_k, lnx_g, lnx_b, v_mix):
    B, T, _ = z.shape
    C, H, N = RWKV_DIM, RWKV_HEADS, RWKV_HEAD
    f32 = jnp.float32
    z = z + mu * (0.5 * (_shift_prev(z) + _shift_next(z)) - z)
    r = z[..., 0:C]
    k = z[..., C:2 * C]
    v = z[..., 2 * C:3 * C]
    o = 3 * C
    zw = z[..., o:o + 2 * DECAY_LORA].reshape(B, T, 2, DECAY_LORA)
    o += 2 * DECAY_LORA
    za = z[..., o:o + 2 * AAA_LORA].reshape(B, T, 2, AAA_LORA)
    o += 2 * AAA_LORA
    zg = z[..., o:o + GATE_LORA]
    decay = jnp.exp(-DECAY_SCALE * jax.nn.sigmoid(
        (w0 + jnp.einsum('btdl,dlc->btdc', jnp.tanh(zw), w_up)).astype(f32)))
    a = jax.nn.sigmoid(a0 + jnp.einsum('btdl,dlc->btdc', za, a_up)).astype(f32)
    g = jax.nn.sigmoid(zg) @ g_up
    if v_mix is None:
        v_first = v
    else:
        v0, v_down, v_upm = v_mix
        v = v + (v_first - v) * jax.nn.sigmoid(v0 + (x @ v_down) @ v_upm)
    kkh = (k * k_k).reshape(B, T, H, N).astype(f32)
    kkh = kkh * lax.rsqrt(jnp.sum(kkh * kkh, -1, keepdims=True) + 1e-12)
    kk = kkh.reshape(B, T, C)
    k_dir = k.astype(f32)[:, :, None, :] * (1.0 + (a - 1.0) * k_a.astype(f32))
    kka = kk[:, :, None, :] * a

    def dirs(u):
        u = jnp.stack([u[:, :, 0], jnp.flip(u[:, :, 1], axis=1)], axis=0).astype(f32)
        return u.reshape(2, B, T, H, N).transpose(2, 0, 1, 3, 4)

    def both(u):
        return dirs(jnp.broadcast_to(u[:, :, None, :], (B, T, 2, C)))

    y = _rwkv_scan(both(r), dirs(decay), dirs(k_dir), both(v), both(kk), dirs(kka))
    y = (y[:, 0] + jnp.flip(y[:, 1], axis=0)).transpose(1, 0, 2, 3)
    mean = jnp.mean(y, -1, keepdims=True)
    var = jnp.mean(jnp.square(y - mean), -1, keepdims=True)
    yn = ((y - mean) * lax.rsqrt(var + GN_EPS)).reshape(B, T, C) * lnx_g.astype(f32) + lnx_b.astype(f32)
    rh = r.reshape(B, T, H, N).astype(f32)
    ksh = jnp.sum(k_dir, axis=2).reshape(B, T, H, N)
    vh = v.reshape(B, T, H, N).astype(f32)
    bonus = jnp.sum(rh * ksh * r_k.reshape(H, N).astype(f32), -1, keepdims=True) * vh
    out = ((yn + bonus.reshape(B, T, C)) * g.astype(f32)).astype(x.dtype)
    return out, v_first


def setup_inputs(seed: int = 0) -> dict:
    key = jax.random.key(seed)
    ks = iter(jax.random.split(key, 48))
    f32 = jnp.float32

    def nrm(shape, scale):
        return jax.random.normal(next(ks), shape, f32) * scale

    def gain(shape):
        return 1.0 + nrm(shape, 0.02)

    L = DEPTH
    C = RWKV_DIM
    return {
        'x_prompt': nrm((BATCH, SEQ, D_MODEL), 1.0),
        'x_sample': nrm((DEC_BATCH, DEC_SEQ, D_MODEL), 1.0),
        'p_prompt': nrm((DEPTH, BATCH, SEQ, P_DIM), 1.0),
        'p_sample': nrm((DEPTH, DEC_BATCH, DEC_SEQ, P_DIM), 1.0),
        'w_in': nrm((L, D_MODEL, IN_COLS), D_MODEL ** -0.5),
        'q_norm_g': gain((L, Q_LORA)),
        'kv_norm_g': gain((L, KV_LORA)),
        'w_uq': nrm((L, Q_LORA, MLA_HEADS * MLA_QK), Q_LORA ** -0.5),
        'w_ukv': nrm((L, KV_LORA, MLA_HEADS * (MLA_NOPE + MLA_V)), KV_LORA ** -0.5),
        'tshift_mu': jax.random.uniform(next(ks), (L, RWKV_COLS), f32),
        'w0': -2.0 + nrm((L, 2, C), 0.5),
        'w_lora_up': nrm((L, 2, DECAY_LORA, C), 0.1),
        'a0': nrm((L, 2, C), 0.5),
        'a_lora_up': nrm((L, 2, AAA_LORA, C), 0.1),
        'g_lora_up': nrm((L, GATE_LORA, C), GATE_LORA ** -0.5),
        'k_k': 0.85 + nrm((L, C), 0.05),
        'k_a': gain((L, C)),
        'r_k': nrm((L, C), 0.1),
        'v0': 1.0 + nrm((L - 1, C), 0.1),
        'v_lora_down': nrm((L - 1, D_MODEL, MV_LORA), D_MODEL ** -0.5),
        'v_lora_up': nrm((L - 1, MV_LORA, C), MV_LORA ** -0.5),
        'lnx_g': gain((L, C)),
        'lnx_b': nrm((L, C), 0.02),
        'w_pa': nrm((L, MLA_HEADS * MLA_V, D_MODEL), (MLA_HEADS * MLA_V) ** -0.5),
        'w_pb': nrm((L, C, D_MODEL), C ** -0.5),
        'w_o': nrm((L, D_MODEL, D_MODEL), D_MODEL ** -0.5 * BETA),
        'ln1_g': gain((L, D_MODEL)),
        'ln1_b': nrm((L, D_MODEL), 0.02),
        'w_ffn_up': nrm((L, D_MODEL, 2 * D_FF), D_MODEL ** -0.5),
        'conv_w': nrm((L, CONV_W, 2 * D_FF), CONV_W ** -0.5),
        'conv_b': nrm((L, 2 * D_FF), 0.02),
        'w_ffn_down': nrm((L, D_FF, D_MODEL), D_FF ** -0.5 * BETA),
        'w_pe_gate': nrm((L, D_MODEL, D_MODEL), D_MODEL ** -0.5),
        'w_pe_proj': nrm((L, P_DIM, D_MODEL), P_DIM ** -0.5 * BETA),
        'ln2_g': gain((L, D_MODEL)),
        'ln2_b': nrm((L, D_MODEL), 0.02),
    }


def reference(x_prompt, x_sample, p_prompt, p_sample, w_in, q_norm_g, kv_norm_g, w_uq, w_ukv,
              tshift_mu, w0, w_lora_up, a0, a_lora_up, g_lora_up, k_k, k_a, r_k, v0, v_lora_down,
              v_lora_up, lnx_g, lnx_b, w_pa, w_pb, w_o, ln1_g, ln1_b, w_ffn_up, conv_w, conv_b,
              w_ffn_down, w_pe_gate, w_pe_proj, ln2_g, ln2_b):
    def trunk(x, p):
        v_first = None
        for i in range(DEPTH):
            h = x @ w_in[i]
            att = _mla(h[..., :OFF_CKV], h[..., OFF_CKV:OFF_KR], h[..., OFF_KR:OFF_RWKV],
                       q_norm_g[i], kv_norm_g[i], w_uq[i], w_ukv[i])
            v_mix = None if i == 0 else (v0[i - 1], v_lora_down[i - 1], v_lora_up[i - 1])
            rw, v_first = _rwkv_time_mix(h[..., OFF_RWKV:OFF_GA], x, v_first, tshift_mu[i], w0[i],
                                         w_lora_up[i], a0[i], a_lora_up[i], g_lora_up[i], k_k[i],
                                         k_a[i], r_k[i], lnx_g[i], lnx_b[i], v_mix)
            mixed = (jax.nn.sigmoid(h[..., OFF_GA:OFF_GB]) * (att @ w_pa[i])
                     + jax.nn.sigmoid(h[..., OFF_GB:]) * (rw @ w_pb[i]))
            x = _layernorm(ALPHA * x + mixed @ w_o[i], ln1_g[i], ln1_b[i])
            u = _dwconv3(x @ w_ffn_up[i], conv_w[i], conv_b[i])
            f = (jax.nn.gelu(u[..., :D_FF]) * u[..., D_FF:]) @ w_ffn_down[i]
            e = jax.nn.sigmoid(x @ w_pe_gate[i]) * (p[i] @ w_pe_proj[i])
            x = _layernorm(ALPHA * x + f + e, ln2_g[i], ln2_b[i])
        return x

    y_prompt = trunk(x_prompt, p_prompt)
    y_sample = trunk(x_sample, p_sample)
    return (y_prompt, y_sample)
```

```python
import functools
import math

import jax
import jax.numpy as jnp
from jax import lax
from jax.experimental import pallas as pl
from jax.experimental.pallas import tpu as pltpu

F32 = jnp.float32
BF16 = jnp.bfloat16

D_MODEL = 1024
HEADS = 8
NOPE = 64
ROPE = 32
HALF = ROPE // 2
VDIM = 64
QK = NOPE + ROPE
HPAD = 128
Q_LORA = 768
KV_LORA = 256
ROPE_THETA = 10000.0
RH = 64
RC = HEADS * RH
LORA = 64
GATE_LORA = 128
MV_LORA = 32
DECAY_SCALE = 0.606531
GN_EPS = 64e-5
D_FF = 2816
FF_BLK = 256
N_FF = D_FF // FF_BLK
P_DIM = 256
LN_EPS = 1e-5
RMS_EPS = 1e-6
CHUNK = 64
GRP = 4
BD = GRP * CHUNK
HALO = 16

OFF_CKV = Q_LORA
OFF_KR = OFF_CKV + KV_LORA
OFF_RWKV = OFF_KR + ROPE
RWKV_COLS = 3 * RC + 2 * LORA + 2 * LORA + GATE_LORA
OFF_GA = OFF_RWKV + RWKV_COLS
OFF_GB = OFF_GA + D_MODEL

VMEM_LIMIT = 56 * 1024 * 1024


def _cparams(sem):
    return pltpu.CompilerParams(dimension_semantics=sem, vmem_limit_bytes=VMEM_LIMIT)


def _dot(a, b):
    return jnp.dot(a, b, preferred_element_type=F32)


def _dot_nt(a, b):
    return lax.dot_general(a, b, (((1,), (1,)), ((), ())), preferred_element_type=F32)


def _dot_tn(a, b):
    return lax.dot_general(a, b, (((0,), (0,)), ((), ())), preferred_element_type=F32)


def _split2(x):
    hi = x.astype(BF16)
    lo = (x - hi.astype(F32)).astype(BF16)
    return hi, lo


def _segsum(x, ones_bd):
    hi, lo = _split2(x)
    return _dot(hi, ones_bd) + _dot(lo, ones_bd)


def _sigmoid(x):
    return 1.0 / (1.0 + jnp.exp(-x))


def _layernorm(x, g, b):
    mu = jnp.mean(x, axis=-1, keepdims=True)
    d = x - mu
    var = jnp.mean(d * d, axis=-1, keepdims=True)
    return d * lax.rsqrt(var + LN_EPS) * g + b


class _Groups:
    def __init__(self, b0, t0, b1, t1):
        self.b = (b0, b1)
        self.t = (t0, t1)
        self.n0 = b0 * t0
        self.n = b0 * t0 + b1 * t1
        self.off = (0, self.n0)

    def seq_pos(self, row0):
        in0 = row0 < self.n0
        pos = jnp.where(in0, lax.rem(row0, self.t[0]), lax.rem(row0 - self.n0, self.t[1]))
        tlen = jnp.where(in0, self.t[0], self.t[1])
        return pos, tlen


def _in_proj_kernel(x_ref, tab_ref, w1_ref, wz_ref, qg_ref, kvg_ref, wuq_ref, wuk_ref, wuv_ref,
                    e_ref, q_ref, k_ref, v_ref, z_ref):
    xb = x_ref[...].astype(BF16)
    h1 = _dot(xb, w1_ref[...])
    z_ref[...] = _dot(xb, wz_ref[...])

    cq = h1[:, :Q_LORA]
    cqn = cq * lax.rsqrt(jnp.mean(cq * cq, axis=-1, keepdims=True) + RMS_EPS) * qg_ref[...]
    q = _dot(cqn.astype(BF16), wuq_ref[...])
    cq_tab = tab_ref[:, 0:HPAD]
    sa_tab = tab_ref[:, HPAD:2 * HPAD]
    sb_tab = tab_ref[:, 2 * HPAD:3 * HPAD]
    for h in range(HEADS):
        qh = q[:, h * HPAD:(h + 1) * HPAD]
        out = qh * cq_tab + pltpu.roll(qh, HALF, 1) * sa_tab + pltpu.roll(qh, HPAD - HALF, 1) * sb_tab
        q_ref[:, h * HPAD:(h + 1) * HPAD] = out.astype(BF16)

    ckv = h1[:, OFF_CKV:OFF_KR]
    ckvn = ckv * lax.rsqrt(jnp.mean(ckv * ckv, axis=-1, keepdims=True) + RMS_EPS) * kvg_ref[...]
    ckvb = ckvn.astype(BF16)
    kr = h1[:, OFF_KR:OFF_KR + HPAD]
    krs = h1[:, OFF_KR + HPAD:OFF_KR + 2 * HPAD]
    krot = kr * tab_ref[:, 3 * HPAD:4 * HPAD] + krs * tab_ref[:, 4 * HPAD:5 * HPAD]
    k = _dot(ckvb, wuk_ref[...]) + _dot(krot.astype(BF16), e_ref[...])
    k_ref[...] = k.astype(BF16)
    v_ref[...] = _dot(ckvb, wuv_ref[...]).astype(BF16)


def _rope_tables(t_max):
    pos = jnp.arange(t_max, dtype=F32)
    inv_freq = ROPE_THETA ** (-jnp.arange(0, ROPE, 2, dtype=F32) / ROPE)
    ang = pos[:, None] * inv_freq[None, :]
    cos, sin = jnp.cos(ang), jnp.sin(ang)
    scale = QK ** -0.5
    zeros = lambda n: jnp.zeros((t_max, n), F32)
    cq = jnp.concatenate([jnp.ones((t_max, NOPE), F32), cos, cos, zeros(HPAD - QK)], 1) * scale
    sa = jnp.concatenate([zeros(NOPE + HALF), sin, zeros(HPAD - QK)], 1) * scale
    sb = jnp.concatenate([zeros(NOPE), -sin, zeros(HALF + HPAD - QK)], 1) * scale
    ck = jnp.concatenate([cos, cos, zeros(HPAD - ROPE)], 1)
    sk = jnp.concatenate([-sin, sin, zeros(HPAD - ROPE)], 1)
    return jnp.concatenate([cq, sa, sb, ck, sk], 1)


def _in_proj(x, tab, w_in_l, q_norm_g, kv_norm_g, w_uq, w_ukv, grp, tm):
    n = grp.n
    w_kr = w_in_l[:, OFF_KR:OFF_RWKV]
    pad = jnp.zeros((D_MODEL, HPAD - ROPE), F32)
    w1 = jnp.concatenate([w_in_l[:, :OFF_KR], w_kr, pad,
                          w_kr[:, HALF:], w_kr[:, :HALF], pad], 1).astype(BF16)
    wz = w_in_l[:, OFF_RWKV:OFF_GA].astype(BF16)
    wuq = jnp.pad(w_uq.reshape(Q_LORA, HEADS, QK), ((0, 0), (0, 0), (0, HPAD - QK)))
    wuq = wuq.reshape(Q_LORA, HEADS * HPAD).astype(BF16)
    wkv = w_ukv.reshape(KV_LORA, HEADS, NOPE + VDIM)
    wuk = jnp.pad(wkv[:, :, :NOPE], ((0, 0), (0, 0), (0, HPAD - NOPE)))
    wuk = wuk.reshape(KV_LORA, HEADS * HPAD).astype(BF16)
    wuv = wkv[:, :, NOPE:].reshape(KV_LORA, HEADS * VDIM).astype(BF16)
    place = (jnp.arange(HEADS * HPAD)[None, :] % HPAD) == (NOPE + jnp.arange(HPAD)[:, None])
    place = jnp.where(jnp.arange(HPAD)[:, None] < ROPE, place, False).astype(BF16)

    bt0, bt1 = grp.t[0] // tm, grp.t[1] // tm
    nb0 = grp.n0 // tm

    def tab_map(i):
        return (jnp.where(i < nb0, lax.rem(i, bt0), lax.rem(i - nb0, bt1)), 0)

    full = lambda a: pl.BlockSpec(a.shape, lambda i: (0,) * a.ndim)
    row = lambda c: pl.BlockSpec((tm, c), lambda i: (i, 0))
    qg = q_norm_g.reshape(1, Q_LORA)
    kvg = kv_norm_g.reshape(1, KV_LORA)
    return pl.pallas_call(
        _in_proj_kernel,
        grid=(n // tm,),
        in_specs=[row(D_MODEL), pl.BlockSpec((tm, 5 * HPAD), tab_map), full(w1), full(wz), full(qg),
                  full(kvg), full(wuq), full(wuk), full(wuv), full(place)],
        out_specs=[row(HEADS * HPAD), row(HEADS * HPAD), row(HEADS * VDIM), row(RWKV_COLS)],
        out_shape=[jax.ShapeDtypeStruct((n, HEADS * HPAD), BF16),
                   jax.ShapeDtypeStruct((n, HEADS * HPAD), BF16),
                   jax.ShapeDtypeStruct((n, HEADS * VDIM), BF16),
                   jax.ShapeDtypeStruct((n, RWKV_COLS), F32)],
        compiler_params=_cparams(("parallel",)),
        name="in_proj",
    )(x, tab, w1, wz, qg, kvg, wuq, wuk, wuv, place)


def _attn_kernel(q_ref, k_ref, v_ref, o_ref, m_ref, l_ref, acc_ref):
    ki = pl.program_id(2)

    @pl.when(ki == 0)
    def _():
        m_ref[...] = jnp.full(m_ref.shape, -jnp.inf, F32)
        l_ref[...] = jnp.zeros(l_ref.shape, F32)
        acc_ref[...] = jnp.zeros(acc_ref.shape, F32)

    for h in range(HEADS):
        s = _dot_nt(q_ref[:, h * HPAD:(h + 1) * HPAD], k_ref[:, h * HPAD:(h + 1) * HPAD])
        m_prev = m_ref[h]
        m_new = jnp.maximum(m_prev, jnp.max(s, axis=-1, keepdims=True))
        alpha = jnp.exp(m_prev - m_new)
        p = jnp.exp(s - m_new)
        l_ref[h] = alpha * l_ref[h] + jnp.sum(p, axis=-1, keepdims=True)
        acc_ref[h] = alpha * acc_ref[h] + _dot(p.astype(BF16), v_ref[:, h * VDIM:(h + 1) * VDIM])
        m_ref[h] = m_new

    @pl.when(ki == pl.num_programs(2) - 1)
    def _():
        for h in range(HEADS):
            o_ref[:, h * VDIM:(h + 1) * VDIM] = (acc_ref[h] / l_ref[h]).astype(BF16)


def _attention(q, k, v, grp, gi, tq, tk):
    b, t = grp.b[gi], grp.t[gi]
    nq, nk = t // tq, t // tk
    qoff, koff = grp.off[gi] // tq, grp.off[gi] // tk
    qmap = lambda bi, qi, ki: (qoff + bi * nq + qi, 0)
    kmap = lambda bi, qi, ki: (koff + bi * nk + ki, 0)
    return pl.pallas_call(
        _attn_kernel,
        grid=(b, nq, nk),
        in_specs=[pl.BlockSpec((tq, HEADS * HPAD), qmap), pl.BlockSpec((tk, HEADS * HPAD), kmap),
                  pl.BlockSpec((tk, HEADS * VDIM), kmap)],
        out_specs=pl.BlockSpec((tq, HEADS * VDIM), lambda bi, qi, ki: (bi * nq + qi, 0)),
        out_shape=jax.ShapeDtypeStruct((b * t, HEADS * VDIM), BF16),
        scratch_shapes=[pltpu.VMEM((HEADS, tq, 1), F32), pltpu.VMEM((HEADS, tq, 1), F32),
                        pltpu.VMEM((HEADS, tq, VDIM), F32)],
        compiler_params=_cparams(("parallel", "parallel", "arbitrary")),
        name=f"attention_g{gi}",
    )(q, k, v)


def _rwkv_prep_kernel(grp, tm, has_vmix, *refs):
    if has_vmix:
        (z_ref, zp_ref, zn_ref, mu_ref, wup_ref, w0_ref, aup_ref, a0_ref, gup_ref, kk_ref, ka_ref,
         rk_ref, ones_ref, x_ref, vf_ref, v0_ref, vd_ref, vu_ref,
         r_out, v_out, kk_out, lw_out, kd_out, kka_out, g_out, bonus_out) = refs
    else:
        (z_ref, zp_ref, zn_ref, mu_ref, wup_ref, w0_ref, aup_ref, a0_ref, gup_ref, kk_ref, ka_ref,
         rk_ref, ones_ref,
         r_out, v_out, kk_out, lw_out, kd_out, kka_out, g_out, bonus_out) = refs
    i = pl.program_id(0)
    pos, tlen = grp.seq_pos(i * tm)
    first = pos == 0
    last = pos + tm == tlen

    zt = z_ref[...]
    prev_row = jnp.where(first, 0.0, zp_ref[7:8, :])
    next_row = jnp.where(last, 0.0, zn_ref[0:1, :])
    rows = lax.broadcasted_iota(jnp.int32, zt.shape, 0)
    z_prev = jnp.where(rows == 0, prev_row, pltpu.roll(zt, 1, 0))
    z_next = jnp.where(rows == tm - 1, next_row, pltpu.roll(zt, tm - 1, 0))
    zm = zt + mu_ref[...] * (0.5 * (z_prev + z_next) - zt)

    r = zm[:, 0:RC]
    k = zm[:, RC:2 * RC]
    v = zm[:, 2 * RC:3 * RC]
    o = 3 * RC
    zw = zm[:, o:o + 2 * LORA]
    za = zm[:, o + 2 * LORA:o + 4 * LORA]
    zg = zm[:, o + 4 * LORA:o + 4 * LORA + GATE_LORA]

    lw2 = -DECAY_SCALE * _sigmoid(w0_ref[...] + _dot(jnp.tanh(zw).astype(BF16), wup_ref[...]))
    a2 = _sigmoid(a0_ref[...] + _dot(za.astype(BF16), aup_ref[...]))
    g = _dot(_sigmoid(zg).astype(BF16), gup_ref[...])
    if has_vmix:
        xd = _dot(x_ref[...].astype(BF16), vd_ref[...])
        gate = _sigmoid(v0_ref[...] + _dot(xd.astype(BF16), vu_ref[...]))
        v = v + (vf_ref[...] - v) * gate

    ones_bd = ones_ref[...]
    kkr = k * kk_ref[...]
    kk = kkr * lax.rsqrt(_segsum(kkr * kkr, ones_bd) + 1e-12)
    ksum = jnp.zeros_like(k)
    for d in range(2):
        a_d = a2[:, d * RC:(d + 1) * RC]
        kd = k * (1.0 + (a_d - 1.0) * ka_ref[...])
        ksum = ksum + kd
        lw_out[d] = lw2[:, d * RC:(d + 1) * RC]
        kd_out[d] = kd
        kka_out[d] = kk * a_d
    r_out[...] = r
    v_out[...] = v
    kk_out[...] = kk
    g_out[...] = g
    bonus_out[...] = _segsum(r * ksum * rk_ref[...], ones_bd) * v


def _blockdiag2(w):
    z = jnp.zeros_like(w[0])
    return jnp.concatenate([jnp.concatenate([w[0], z], 1), jnp.concatenate([z, w[1]], 1)], 0)


def _ones_bd():
    idx = jnp.arange(RC) // RH
    return (idx[:, None] == idx[None, :]).astype(BF16)


def _rwkv_prep(z, x, v_first, p, layer, grp, tm):
    n = grp.n
    has_vmix = layer > 0
    row2 = lambda a: a.reshape(1, -1)
    mu = row2(p['tshift_mu'][layer])
    wup = _blockdiag2(p['w_lora_up'][layer]).astype(BF16)
    w0 = row2(p['w0'][layer])
    aup = _blockdiag2(p['a_lora_up'][layer]).astype(BF16)
    a0 = row2(p['a0'][layer])
    gup = p['g_lora_up'][layer].astype(BF16)
    kk_p, ka_p, rk_p = row2(p['k_k'][layer]), row2(p['k_a'][layer]), row2(p['r_k'][layer])
    ones_bd = _ones_bd()

    nb8 = n // 8
    full = lambda a: pl.BlockSpec(a.shape, lambda i: (0,) * a.ndim)
    row = lambda c: pl.BlockSpec((tm, c), lambda i: (i, 0))
    row_d = pl.BlockSpec((2, tm, RC), lambda i: (0, i, 0))
    in_specs = [row(RWKV_COLS),
                pl.BlockSpec((8, RWKV_COLS), lambda i: (jnp.maximum(i * (tm // 8) - 1, 0), 0)),
                pl.BlockSpec((8, RWKV_COLS), lambda i: (jnp.minimum((i + 1) * (tm // 8), nb8 - 1), 0)),
                full(mu), full(wup), full(w0), full(aup), full(a0), full(gup), full(kk_p), full(ka_p),
                full(rk_p), full(ones_bd)]
    args = [z, z, z, mu, wup, w0, aup, a0, gup, kk_p, ka_p, rk_p, ones_bd]
    if has_vmix:
        v0 = row2(p['v0'][layer - 1])
        vd = jnp.pad(p['v_lora_down'][layer - 1], ((0, 0), (0, HPAD - MV_LORA))).astype(BF16)
        vu = jnp.pad(p['v_lora_up'][layer - 1], ((0, HPAD - MV_LORA), (0, 0))).astype(BF16)
        in_specs += [row(D_MODEL), row(RC), full(v0), full(vd), full(vu)]
        args += [x, v_first, v0, vd, vu]
    one = jax.ShapeDtypeStruct((n, RC), F32)
    two = jax.ShapeDtypeStruct((2, n, RC), F32)
    return pl.pallas_call(
        functools.partial(_rwkv_prep_kernel, grp, tm, has_vmix),
        grid=(n // tm,),
        in_specs=in_specs,
        out_specs=[row(RC), row(RC), row(RC), row_d, row_d, row_d, row(RC), row(RC)],
        out_shape=[one, one, one, two, two, two, one, one],
        compiler_params=_cparams(("parallel",)),
        name=f"rwkv_prep_l{layer}",
    )(*args)


def _scan_kernel(r_ref, v_ref, kk_ref, lw_ref, kd_ref, kka_ref, y_ref, s_ref):
    d = pl.program_id(1)
    c = pl.program_id(2)

    @pl.when(c == 0)
    def _():
        s_ref[...] = jnp.zeros(s_ref.shape, F32)

    sgn = jnp.where(d == 0, 1, -1)
    t_i = lax.broadcasted_iota(jnp.int32, (CHUNK, CHUNK), 0)
    t_j = lax.broadcasted_iota(jnp.int32, (CHUNK, CHUNK), 1)
    cum_mask = jnp.where((t_i - t_j) * sgn >= 0, 1.0, 0.0).astype(BF16)

    bi = lax.broadcasted_iota(jnp.int32, (BD, BD), 0)
    bj = lax.broadcasted_iota(jnp.int32, (BD, BD), 1)
    shift = int(math.log2(CHUNK))
    same_head = (bi >> shift) == (bj >> shift)
    lag = ((bi & (CHUNK - 1)) - (bj & (CHUNK - 1))) * sgn
    strict = same_head & (lag > 0)
    incl = same_head & (lag >= 0)
    eye = jnp.where(bi == bj, 1.0, 0.0)

    def bd(x):
        return jnp.where(same_head, jnp.concatenate([x] * GRP, axis=0), 0.0).astype(BF16)

    for g in range(HEADS // GRP):
        cols = slice(g * BD, (g + 1) * BD)
        lw = lw_ref[:, cols]
        hi = lw.astype(BF16)
        r1 = lw - hi.astype(F32)
        mid = r1.astype(BF16)
        lo = (r1 - mid.astype(F32)).astype(BF16)
        cs = _dot(cum_mask, hi) + _dot(cum_mask, mid) + _dot(cum_mask, lo)
        gin = jnp.exp(-cs)
        a_bd = bd(jnp.exp(cs - lw) * kk_ref[:, cols])
        b_bd = bd(-kka_ref[:, cols] * gin)
        k_bd = bd(kd_ref[:, cols] * gin)
        q_bd = bd(jnp.exp(cs) * r_ref[:, cols])
        v_bd = bd(v_ref[:, cols])
        g_last = jnp.exp(jnp.sum(lw, axis=0, keepdims=True))

        l_ab = jnp.where(strict, _dot_nt(a_bd, b_bd), 0.0)
        l_ak = jnp.where(strict, _dot_nt(a_bd, k_bd), 0.0)
        l_qb = jnp.where(incl, _dot_nt(q_bd, b_bd), 0.0)
        l_qk = jnp.where(incl, _dot_nt(q_bd, k_bd), 0.0)

        pw = l_ab
        tinv = eye + pw
        for _ in range(int(math.log2(CHUNK)) - 1):
            pwb = pw.astype(BF16)
            pw = _dot(pwb, pwb)
            tinv = tinv + _dot(tinv.astype(BF16), pw.astype(BF16))

        s0 = s_ref[g]
        s0b = s0.astype(BF16)
        rhs = _dot_nt(a_bd, s0b) + _dot(l_ak.astype(BF16), v_bd)
        u = _dot(tinv.astype(BF16), rhs.astype(BF16))
        ub = u.astype(BF16)
        y = _dot_nt(q_bd, s0b) + _dot(l_qb.astype(BF16), ub) + _dot(l_qk.astype(BF16), v_bd)
        y_ref[:, cols] = y[0:CHUNK] + y[CHUNK:2 * CHUNK] + y[2 * CHUNK:3 * CHUNK] + y[3 * CHUNK:4 * CHUNK]
        s_ref[g] = (s0 + _dot_tn(ub, b_bd) + _dot_tn(v_bd, k_bd)) * g_last


def _rwkv_scan(r, v, kk, lw2, kd2, kka2, grp, gi):
    b, t = grp.b[gi], grp.t[gi]
    nc = t // CHUNK
    off = grp.off[gi] // CHUNK

    def blk(bi, d, c):
        return bi * nc + jnp.where(d == 0, c, nc - 1 - c)

    one = pl.BlockSpec((CHUNK, RC), lambda bi, d, c: (off + blk(bi, d, c), 0))
    two = pl.BlockSpec((None, CHUNK, RC), lambda bi, d, c: (d, off + blk(bi, d, c), 0))
    return pl.pallas_call(
        _scan_kernel,
        grid=(b, 2, nc),
        in_specs=[one, one, one, two, two, two],
        out_specs=pl.BlockSpec((None, CHUNK, RC), lambda bi, d, c: (d, blk(bi, d, c), 0)),
        out_shape=jax.ShapeDtypeStruct((2, b * t, RC), F32),
        scratch_shapes=[pltpu.VMEM((HEADS // GRP, BD, BD), F32)],
        compiler_params=_cparams(("parallel", "arbitrary", "arbitrary")),
        name=f"rwkv_scan_g{gi}",
    )(r, v, kk, lw2, kd2, kka2)


def _mix_kernel(alpha, x_ref, att_ref, y_ref, g_ref, bonus_ref, lng_ref, lnb_ref, ones_ref, wg_ref,
                wpa_ref, wpb_ref, wo_ref, g1_ref, b1_ref, o_ref):
    ones_bd = ones_ref[...]
    y = y_ref[0] + y_ref[1]
    mean = _segsum(y, ones_bd) * (1.0 / RH)
    dlt = y - mean
    var = _segsum(dlt * dlt, ones_bd) * (1.0 / RH)
    yn = dlt * lax.rsqrt(var + GN_EPS) * lng_ref[...] + lnb_ref[...]
    rw = ((yn + bonus_ref[...]) * g_ref[...]).astype(BF16)

    x = x_ref[...]
    gates = _dot(x.astype(BF16), wg_ref[...])
    mixed = (_sigmoid(gates[:, :D_MODEL]) * _dot(att_ref[...], wpa_ref[...])
             + _sigmoid(gates[:, D_MODEL:]) * _dot(rw, wpb_ref[...]))
    o_ref[...] = _layernorm(alpha * x + _dot(mixed.astype(BF16), wo_ref[...]), g1_ref[...], b1_ref[...])


def _mix(x, att, y2, g, bonus, p, w_in_l, layer, grp, tm, alpha):
    n = grp.n
    row2 = lambda a: a.reshape(1, -1)
    lng, lnb = row2(p['lnx_g'][layer]), row2(p['lnx_b'][layer])
    ones_bd = _ones_bd()
    wg = w_in_l[:, OFF_GA:].astype(BF16)
    wpa, wpb, wo = (p[nm][layer].astype(BF16) for nm in ('w_pa', 'w_pb', 'w_o'))
    g1, b1 = row2(p['ln1_g'][layer]), row2(p['ln1_b'][layer])
    full = lambda a: pl.BlockSpec(a.shape, lambda i: (0,) * a.ndim)
    row = lambda c: pl.BlockSpec((tm, c), lambda i: (i, 0))
    return pl.pallas_call(
        functools.partial(_mix_kernel, alpha),
        grid=(n // tm,),
        in_specs=[row(D_MODEL), row(RC), pl.BlockSpec((2, tm, RC), lambda i: (0, i, 0)), row(RC), row(RC),
                  full(lng), full(lnb), full(ones_bd), full(wg), full(wpa), full(wpb), full(wo),
                  full(g1), full(b1)],
        out_specs=row(D_MODEL),
        out_shape=jax.ShapeDtypeStruct((n, D_MODEL), F32),
        compiler_params=_cparams(("parallel",)),
        name=f"mix_l{layer}",
    )(x, att, y2, g, bonus, lng, lnb, ones_bd, wg, wpa, wpb, wo, g1, b1)


def _ffn_kernel(grp, tm, alpha, x_ref, xp_ref, xn_ref, p_ref, wup_ref, cw_ref, cb_ref, wdn_ref, wpg_ref,
                wpp_ref, g2_ref, b2_ref, o_ref, xe_ref, ue_ref, acc_ref):
    i = pl.program_id(0)
    j = pl.program_id(1)

    @pl.when(j == 0)
    def _():
        pos, tlen = grp.seq_pos(i * tm)
        keep_prev = (pos != 0).astype(F32)
        keep_next = (pos + tm != tlen).astype(F32)
        xe_ref[0:HALO, :] = (xp_ref[...] * keep_prev).astype(BF16)
        xe_ref[HALO:HALO + tm, :] = x_ref[...].astype(BF16)
        xe_ref[HALO + tm:, :] = (xn_ref[...] * keep_next).astype(BF16)
        acc_ref[...] = jnp.zeros(acc_ref.shape, F32)

    ue_ref[...] = _dot(xe_ref[...], wup_ref[...])
    cw = cw_ref[...]
    u = (cw[0:1] * ue_ref[pl.ds(HALO - 1, tm), :] + cw[1:2] * ue_ref[pl.ds(HALO, tm), :]
         + cw[2:3] * ue_ref[pl.ds(HALO + 1, tm), :] + cb_ref[...])
    hg = jax.nn.gelu(u[:, :FF_BLK]) * u[:, FF_BLK:]
    acc_ref[...] += _dot(hg.astype(BF16), wdn_ref[...])

    @pl.when(j == pl.num_programs(1) - 1)
    def _():
        x = x_ref[...]
        e = (_sigmoid(_dot(xe_ref[HALO:HALO + tm, :], wpg_ref[...]))
             * _dot(p_ref[...].astype(BF16), wpp_ref[...]))
        o_ref[...] = _layernorm(alpha * x + acc_ref[...] + e, g2_ref[...], b2_ref[...])


def _ffn(x, pe, p, layer, grp, tm, alpha):
    n = grp.n
    row2 = lambda a: a.reshape(1, -1)

    def interleave(a):
        lead = a.shape[:-1]
        a = a.reshape(lead + (2, N_FF, FF_BLK))
        a = jnp.moveaxis(a, -2, 0)
        return a.reshape((N_FF,) + lead + (2 * FF_BLK,))

    wup = interleave(p['w_ffn_up'][layer]).astype(BF16)
    cw = jnp.pad(interleave(p['conv_w'][layer]), ((0, 0), (0, 5), (0, 0)))
    cb = interleave(p['conv_b'][layer].reshape(1, -1))
    wdn = p['w_ffn_down'][layer].reshape(N_FF, FF_BLK, D_MODEL).astype(BF16)
    wpg = p['w_pe_gate'][layer].astype(BF16)
    wpp = p['w_pe_proj'][layer].astype(BF16)
    g2, b2 = row2(p['ln2_g'][layer]), row2(p['ln2_b'][layer])

    nbh = n // HALO
    full = lambda a: pl.BlockSpec(a.shape, lambda i, j: (0,) * a.ndim)
    blk = lambda a: pl.BlockSpec((None,) + a.shape[1:], lambda i, j: (j,) + (0,) * (a.ndim - 1))
    row = lambda c: pl.BlockSpec((tm, c), lambda i, j: (i, 0))
    return pl.pallas_call(
        functools.partial(_ffn_kernel, grp, tm, alpha),
        grid=(n // tm, N_FF),
        in_specs=[row(D_MODEL),
                  pl.BlockSpec((HALO, D_MODEL), lambda i, j: (jnp.maximum(i * (tm // HALO) - 1, 0), 0)),
                  pl.BlockSpec((HALO, D_MODEL), lambda i, j: (jnp.minimum((i + 1) * (tm // HALO), nbh - 1), 0)),
                  row(P_DIM), blk(wup), blk(cw), blk(cb), blk(wdn), full(wpg), full(wpp), full(g2), full(b2)],
        out_specs=row(D_MODEL),
        out_shape=jax.ShapeDtypeStruct((n, D_MODEL), F32),
        scratch_shapes=[pltpu.VMEM((tm + 2 * HALO, D_MODEL), BF16),
                        pltpu.VMEM((tm + 2 * HALO, 2 * FF_BLK), F32),
                        pltpu.VMEM((tm, D_MODEL), F32)],
        compiler_params=_cparams(("parallel", "arbitrary")),
        name=f"ffn_l{layer}",
    )(x, x, x, pe, wup, cw, cb, wdn, wpg, wpp, g2, b2)


def kernel(x_prompt, x_sample, p_prompt, p_sample, w_in, q_norm_g, kv_norm_g, w_uq, w_ukv, tshift_mu, w0, w_lora_up, a0, a_lora_up, g_lora_up, k_k, k_a, r_k, v0, v_lora_down, v_lora_up, lnx_g, lnx_b, w_pa, w_pb, w_o, ln1_g, ln1_b, w_ffn_up, conv_w, conv_b, w_ffn_down, w_pe_gate, w_pe_proj, ln2_g, ln2_b):
    depth = w_in.shape[0]
    alpha = (2 * depth) ** 0.25
    b0, t0, _ = x_prompt.shape
    b1, t1, _ = x_sample.shape
    grp = _Groups(b0, t0, b1, t1)
    n = grp.n
    params = dict(tshift_mu=tshift_mu, w0=w0, w_lora_up=w_lora_up, a0=a0, a_lora_up=a_lora_up,
                  g_lora_up=g_lora_up, k_k=k_k, k_a=k_a, r_k=r_k, v0=v0, v_lora_down=v_lora_down,
                  v_lora_up=v_lora_up, lnx_g=lnx_g, lnx_b=lnx_b, w_pa=w_pa, w_pb=w_pb, w_o=w_o,
                  ln1_g=ln1_g, ln1_b=ln1_b, w_ffn_up=w_ffn_up, conv_w=conv_w, conv_b=conv_b,
                  w_ffn_down=w_ffn_down, w_pe_gate=w_pe_gate, w_pe_proj=w_pe_proj, ln2_g=ln2_g,
                  ln2_b=ln2_b)

    tmin = min(t0, t1)
    tm = min(256, tmin)
    tq = min(512, tmin)
    x = jnp.concatenate([x_prompt.reshape(b0 * t0, D_MODEL), x_sample.reshape(b1 * t1, D_MODEL)], 0)
    pe = jnp.concatenate([p_prompt.reshape(depth, b0 * t0, P_DIM), p_sample.reshape(depth, b1 * t1, P_DIM)], 1)
    tab = _rope_tables(max(t0, t1))

    v_first = None
    for layer in range(depth):
        q, k, v, z = _in_proj(x, tab, w_in[layer], q_norm_g[layer], kv_norm_g[layer], w_uq[layer],
                              w_ukv[layer], grp, tm)
        att = jnp.concatenate([_attention(q, k, v, grp, gi, tq, tq) for gi in range(2)], 0)
        r, vv, kk, lw2, kd2, kka2, g, bonus = _rwkv_prep(z, x, v_first, params, layer, grp, tm)
        if layer == 0:
            v_first = vv
        y2 = jnp.concatenate([_rwkv_scan(r, vv, kk, lw2, kd2, kka2, grp, gi) for gi in range(2)], 1)
        x = _mix(x, att, y2, g, bonus, params, w_in[layer], layer, grp, tm, alpha)
        x = _ffn(x, pe[layer], params, layer, grp, tm, alpha)

    return (x[:grp.n0].reshape(b0, t0, D_MODEL), x[grp.n0:].reshape(b1, t1, D_MODEL))
```

```python
import functools
import math

import jax
import jax.numpy as jnp
from jax import lax
from jax.experimental import pallas as pl
from jax.experimental.pallas import tpu as pltpu

F32 = jnp.float32
BF16 = jnp.bfloat16

D_MODEL = 1024
HEADS = 8
NOPE = 64
ROPE = 32
HALF = ROPE // 2
VDIM = 64
QK = NOPE + ROPE
HPAD = 128
Q_LORA = 768
KV_LORA = 256
ROPE_THETA = 10000.0
RH = 64
RC = HEADS * RH
LORA = 64
GATE_LORA = 128
MV_LORA = 32
DECAY_SCALE = 0.606531
GN_EPS = 64e-5
D_FF = 2816
FF_BLK = 256
N_FF = D_FF // FF_BLK
P_DIM = 256
LN_EPS = 1e-5
RMS_EPS = 1e-6
CHUNK = 64
GRP = 4
BD = GRP * CHUNK
SCAN_SEQS = 4
HALO = 16
ATT_SEGS = 4
ATT_ONES = 16

OFF_CKV = Q_LORA
OFF_KR = OFF_CKV + KV_LORA
OFF_RWKV = OFF_KR + ROPE
RWKV_COLS = 3 * RC + 2 * LORA + 2 * LORA + GATE_LORA
OFF_GA = OFF_RWKV + RWKV_COLS
OFF_GB = OFF_GA + D_MODEL

VMEM_LIMIT = 56 * 1024 * 1024


def _cparams(sem):
    return pltpu.CompilerParams(dimension_semantics=sem, vmem_limit_bytes=VMEM_LIMIT)


def _dot(a, b):
    return jnp.dot(a, b, preferred_element_type=F32)


def _dot_nt(a, b):
    return lax.dot_general(a, b, (((1,), (1,)), ((), ())), preferred_element_type=F32)


def _dot_tn(a, b):
    return lax.dot_general(a, b, (((0,), (0,)), ((), ())), preferred_element_type=F32)


def _split2(x):
    hi = x.astype(BF16)
    lo = (x - hi.astype(F32)).astype(BF16)
    return hi, lo


def _segsum(x, ones_bd):
    hi, lo = _split2(x)
    return _dot(hi, ones_bd) + _dot(lo, ones_bd)


def _sigmoid(x):
    return 1.0 / (1.0 + jnp.exp(-x))


def _layernorm(x, g, b):
    mu = jnp.mean(x, axis=-1, keepdims=True)
    d = x - mu
    var = jnp.mean(d * d, axis=-1, keepdims=True)
    return d * lax.rsqrt(var + LN_EPS) * g + b


class _Groups:
    def __init__(self, b0, t0, b1, t1):
        self.b = (b0, b1)
        self.t = (t0, t1)
        self.n0 = b0 * t0
        self.n = b0 * t0 + b1 * t1
        self.off = (0, self.n0)

    def seq_pos(self, row0):
        in0 = row0 < self.n0
        pos = jnp.where(in0, lax.rem(row0, self.t[0]), lax.rem(row0 - self.n0, self.t[1]))
        tlen = jnp.where(in0, self.t[0], self.t[1])
        return pos, tlen


def _in_proj_kernel(x_ref, tab_ref, w1_ref, wz_ref, qg_ref, kvg_ref, wuq_ref, wuk_ref, wuv_ref,
                    e_ref, q_ref, k_ref, v_ref, z_ref):
    xb = x_ref[...].astype(BF16)
    h1 = _dot(xb, w1_ref[...])
    z_ref[...] = _dot(xb, wz_ref[...])

    cq = h1[:, :Q_LORA]
    cqn = cq * lax.rsqrt(jnp.mean(cq * cq, axis=-1, keepdims=True) + RMS_EPS) * qg_ref[...]
    q = _dot(cqn.astype(BF16), wuq_ref[...])
    cq_tab = tab_ref[:, 0:HPAD]
    sa_tab = tab_ref[:, HPAD:2 * HPAD]
    sb_tab = tab_ref[:, 2 * HPAD:3 * HPAD]
    for h in range(HEADS):
        qh = q[:, h * HPAD:(h + 1) * HPAD]
        out = qh * cq_tab + pltpu.roll(qh, HALF, 1) * sa_tab + pltpu.roll(qh, HPAD - HALF, 1) * sb_tab
        q_ref[:, h * HPAD:(h + 1) * HPAD] = out.astype(BF16)

    ckv = h1[:, OFF_CKV:OFF_KR]
    ckvn = ckv * lax.rsqrt(jnp.mean(ckv * ckv, axis=-1, keepdims=True) + RMS_EPS) * kvg_ref[...]
    ckvb = ckvn.astype(BF16)
    kr = h1[:, OFF_KR:OFF_KR + HPAD]
    krs = h1[:, OFF_KR + HPAD:OFF_KR + 2 * HPAD]
    krot = kr * tab_ref[:, 3 * HPAD:4 * HPAD] + krs * tab_ref[:, 4 * HPAD:5 * HPAD]
    k = _dot(ckvb, wuk_ref[...]) + _dot(krot.astype(BF16), e_ref[...])
    k_ref[...] = k.astype(BF16)
    v_ref[...] = _dot_nt(wuv_ref[...], ckvb).astype(BF16)


def _rope_tables(t_max):
    pos = jnp.arange(t_max, dtype=F32)
    inv_freq = ROPE_THETA ** (-jnp.arange(0, ROPE, 2, dtype=F32) / ROPE)
    ang = pos[:, None] * inv_freq[None, :]
    cos, sin = jnp.cos(ang), jnp.sin(ang)
    scale = QK ** -0.5 * math.log2(math.e)
    zeros = lambda n: jnp.zeros((t_max, n), F32)
    cq = jnp.concatenate([jnp.ones((t_max, NOPE), F32), cos, cos, zeros(HPAD - QK)], 1) * scale
    sa = jnp.concatenate([zeros(NOPE + HALF), sin, zeros(HPAD - QK)], 1) * scale
    sb = jnp.concatenate([zeros(NOPE), -sin, zeros(HALF + HPAD - QK)], 1) * scale
    ck = jnp.concatenate([cos, cos, zeros(HPAD - ROPE)], 1)
    sk = jnp.concatenate([-sin, sin, zeros(HPAD - ROPE)], 1)
    return jnp.concatenate([cq, sa, sb, ck, sk], 1)


def _in_proj(x, tab, w_in_l, q_norm_g, kv_norm_g, w_uq, w_ukv, grp, tm):
    n = grp.n
    w_kr = w_in_l[:, OFF_KR:OFF_RWKV]
    pad = jnp.zeros((D_MODEL, HPAD - ROPE), F32)
    w1 = jnp.concatenate([w_in_l[:, :OFF_KR], w_kr, pad,
                          w_kr[:, HALF:], w_kr[:, :HALF], pad], 1).astype(BF16)
    wz = w_in_l[:, OFF_RWKV:OFF_GA].astype(BF16)
    wuq = jnp.pad(w_uq.reshape(Q_LORA, HEADS, QK), ((0, 0), (0, 0), (0, HPAD - QK)))
    wuq = wuq.reshape(Q_LORA, HEADS * HPAD).astype(BF16)
    wkv = w_ukv.reshape(KV_LORA, HEADS, NOPE + VDIM)
    wuk = jnp.pad(wkv[:, :, :NOPE], ((0, 0), (0, 0), (0, HPAD - NOPE)))
    wuk = wuk.reshape(KV_LORA, HEADS * HPAD).astype(BF16)
    wuv = wkv[:, :, NOPE:].reshape(KV_LORA, HEADS * VDIM).T.astype(BF16)
    place = (jnp.arange(HEADS * HPAD)[None, :] % HPAD) == (NOPE + jnp.arange(HPAD)[:, None])
    place = jnp.where(jnp.arange(HPAD)[:, None] < ROPE, place, False).astype(BF16)

    bt0, bt1 = grp.t[0] // tm, grp.t[1] // tm
    nb0 = grp.n0 // tm

    def tab_map(i):
        return (jnp.where(i < nb0, lax.rem(i, bt0), lax.rem(i - nb0, bt1)), 0)

    full = lambda a: pl.BlockSpec(a.shape, lambda i: (0,) * a.ndim)
    row = lambda c: pl.BlockSpec((tm, c), lambda i: (i, 0))
    qg = q_norm_g.reshape(1, Q_LORA)
    kvg = kv_norm_g.reshape(1, KV_LORA)
    return pl.pallas_call(
        _in_proj_kernel,
        grid=(n // tm,),
        in_specs=[row(D_MODEL), pl.BlockSpec((tm, 5 * HPAD), tab_map), full(w1), full(wz), full(qg),
                  full(kvg), full(wuq), full(wuk), full(wuv), full(place)],
        out_specs=[row(HEADS * HPAD), row(HEADS * HPAD),
                   pl.BlockSpec((HEADS * VDIM, tm), lambda i: (0, i)), row(RWKV_COLS)],
        out_shape=[jax.ShapeDtypeStruct((n, HEADS * HPAD), BF16),
                   jax.ShapeDtypeStruct((n, HEADS * HPAD), BF16),
                   jax.ShapeDtypeStruct((HEADS * VDIM, n), BF16),
                   jax.ShapeDtypeStruct((n, RWKV_COLS), F32)],
        compiler_params=_cparams(("parallel",)),
        name="in_proj",
    )(x, tab, w1, wz, qg, kvg, wuq, wuk, wuv, place)


def _attn_kernel(q_ref, k_ref, vt_ref, o_ref, m_ref, acc_ref):
    ki = pl.program_id(2)

    @pl.when(ki == 0)
    def _():
        m_ref[...] = jnp.full(m_ref.shape, -jnp.inf, F32)
        acc_ref[...] = jnp.zeros(acc_ref.shape, F32)

    tk, tq = k_ref.shape[0], q_ref.shape[0]
    seg = ATT_SEGS if tk % (8 * ATT_SEGS) == 0 else 1

    def colreduce(fn, x):
        part = fn(x.reshape(seg, tk // (8 * seg), 8, tq), axis=1)
        return fn(fn(part, axis=0), axis=0, keepdims=True)

    def scores(h):
        hs = slice(h * HPAD, (h + 1) * HPAD)
        return _dot_nt(k_ref[:, hs], q_ref[:, hs])

    ones_rows = (lax.broadcasted_iota(jnp.int32, (ATT_ONES, tk), 0) == 0).astype(BF16)
    st_next = scores(0)
    for h in range(HEADS):
        st = st_next
        if h + 1 < HEADS:
            st_next = scores(h + 1)
        m_prev = m_ref[h]
        m_new = jnp.maximum(m_prev, colreduce(jnp.max, st))
        alpha = jnp.exp2(m_prev - m_new)
        p = jnp.exp2(st - m_new).astype(BF16)
        m_ref[h] = m_new
        vt1 = jnp.concatenate([vt_ref[h * VDIM:(h + 1) * VDIM, :], ones_rows], axis=0)
        acc_ref[h] = alpha * acc_ref[h] + _dot(vt1, p)

    @pl.when(ki == pl.num_programs(2) - 1)
    def _():
        for h in range(HEADS):
            acc = acc_ref[h]
            o = acc[:VDIM] / acc[VDIM:VDIM + 1]
            o_ref[:, h * VDIM:(h + 1) * VDIM] = o.T.astype(BF16)


def _attention(q, k, vt, grp, gi, tq, tk):
    b, t = grp.b[gi], grp.t[gi]
    nq, nk = t // tq, t // tk
    qoff, koff = grp.off[gi] // tq, grp.off[gi] // tk
    qmap = lambda bi, qi, ki: (qoff + bi * nq + qi, 0)
    kmap = lambda bi, qi, ki: (koff + bi * nk + ki, 0)
    return pl.pallas_call(
        _attn_kernel,
        grid=(b, nq, nk),
        in_specs=[pl.BlockSpec((tq, HEADS * HPAD), qmap), pl.BlockSpec((tk, HEADS * HPAD), kmap),
                  pl.BlockSpec((HEADS * VDIM, tk), lambda bi, qi, ki: (0, koff + bi * nk + ki))],
        out_specs=pl.BlockSpec((tq, HEADS * VDIM), lambda bi, qi, ki: (bi * nq + qi, 0)),
        out_shape=jax.ShapeDtypeStruct((b * t, HEADS * VDIM), BF16),
        scratch_shapes=[pltpu.VMEM((HEADS, 1, tq), F32), pltpu.VMEM((HEADS, VDIM + ATT_ONES, tq), F32)],
        compiler_params=_cparams(("parallel", "parallel", "arbitrary")),
        name=f"attention_g{gi}",
    )(q, k, vt)


def _rwkv_prep_kernel(grp, tm, has_vmix, *refs):
    if has_vmix:
        (z_ref, zp_ref, zn_ref, mu_ref, wup_ref, w0_ref, aup_ref, a0_ref, gup_ref, kk_ref, ka_ref,
         rk_ref, ones_ref, x_ref, vf_ref, v0_ref, vd_ref, vu_ref,
         r_out, v_out, kk_out, lw_out, kd_out, kka_out, g_out, bonus_out) = refs
    else:
        (z_ref, zp_ref, zn_ref, mu_ref, wup_ref, w0_ref, aup_ref, a0_ref, gup_ref, kk_ref, ka_ref,
         rk_ref, ones_ref,
         r_out, v_out, kk_out, lw_out, kd_out, kka_out, g_out, bonus_out) = refs
    i = pl.program_id(0)
    pos, tlen = grp.seq_pos(i * tm)
    first = pos == 0
    last = pos + tm == tlen

    zt = z_ref[...]
    prev_row = jnp.where(first, 0.0, zp_ref[7:8, :])
    next_row = jnp.where(last, 0.0, zn_ref[0:1, :])
    rows = lax.broadcasted_iota(jnp.int32, zt.shape, 0)
    z_prev = jnp.where(rows == 0, prev_row, pltpu.roll(zt, 1, 0))
    z_next = jnp.where(rows == tm - 1, next_row, pltpu.roll(zt, tm - 1, 0))
    zm = zt + mu_ref[...] * (0.5 * (z_prev + z_next) - zt)

    r = zm[:, 0:RC]
    k = zm[:, RC:2 * RC]
    v = zm[:, 2 * RC:3 * RC]
    o = 3 * RC
    zw = zm[:, o:o + 2 * LORA]
    za = zm[:, o + 2 * LORA:o + 4 * LORA]
    zg = zm[:, o + 4 * LORA:o + 4 * LORA + GATE_LORA]

    lw2 = -DECAY_SCALE * _sigmoid(w0_ref[...] + _dot(jnp.tanh(zw).astype(BF16), wup_ref[...]))
    a2 = _sigmoid(a0_ref[...] + _dot(za.astype(BF16), aup_ref[...]))
    g = _dot(_sigmoid(zg).astype(BF16), gup_ref[...])
    if has_vmix:
        xd = _dot(x_ref[...].astype(BF16), vd_ref[...])
        gate = _sigmoid(v0_ref[...] + _dot(xd.astype(BF16), vu_ref[...]))
        v = v + (vf_ref[...] - v) * gate

    ones_bd = ones_ref[...]
    kkr = k * kk_ref[...]
    kk = kkr * lax.rsqrt(_segsum(kkr * kkr, ones_bd) + 1e-12)
    ksum = jnp.zeros_like(k)
    for d in range(2):
        a_d = a2[:, d * RC:(d + 1) * RC]
        kd = k * (1.0 + (a_d - 1.0) * ka_ref[...])
        ksum = ksum + kd
        lw_out[d] = lw2[:, d * RC:(d + 1) * RC]
        kd_out[d] = kd
        kka_out[d] = kk * a_d
    r_out[...] = r
    v_out[...] = v
    kk_out[...] = kk
    g_out[...] = g
    bonus_out[...] = _segsum(r * ksum * rk_ref[...], ones_bd) * v


def _blockdiag2(w):
    z = jnp.zeros_like(w[0])
    return jnp.concatenate([jnp.concatenate([w[0], z], 1), jnp.concatenate([z, w[1]], 1)], 0)


def _ones_bd():
    idx = jnp.arange(RC) // RH
    return (idx[:, None] == idx[None, :]).astype(BF16)


def _rwkv_prep(z, x, v_first, p, layer, grp, tm):
    n = grp.n
    has_vmix = layer > 0
    row2 = lambda a: a.reshape(1, -1)
    mu = row2(p['tshift_mu'][layer])
    wup = _blockdiag2(p['w_lora_up'][layer]).astype(BF16)
    w0 = row2(p['w0'][layer])
    aup = _blockdiag2(p['a_lora_up'][layer]).astype(BF16)
    a0 = row2(p['a0'][layer])
    gup = p['g_lora_up'][layer].astype(BF16)
    kk_p, ka_p, rk_p = row2(p['k_k'][layer]), row2(p['k_a'][layer]), row2(p['r_k'][layer])
    ones_bd = _ones_bd()

    nb8 = n // 8
    full = lambda a: pl.BlockSpec(a.shape, lambda i: (0,) * a.ndim)
    row = lambda c: pl.BlockSpec((tm, c), lambda i: (i, 0))
    row_d = pl.BlockSpec((2, tm, RC), lambda i: (0, i, 0))
    in_specs = [row(RWKV_COLS),
                pl.BlockSpec((8, RWKV_COLS), lambda i: (jnp.maximum(i * (tm // 8) - 1, 0), 0)),
                pl.BlockSpec((8, RWKV_COLS), lambda i: (jnp.minimum((i + 1) * (tm // 8), nb8 - 1), 0)),
                full(mu), full(wup), full(w0), full(aup), full(a0), full(gup), full(kk_p), full(ka_p),
                full(rk_p), full(ones_bd)]
    args = [z, z, z, mu, wup, w0, aup, a0, gup, kk_p, ka_p, rk_p, ones_bd]
    if has_vmix:
        v0 = row2(p['v0'][layer - 1])
        vd = jnp.pad(p['v_lora_down'][layer - 1], ((0, 0), (0, HPAD - MV_LORA))).astype(BF16)
        vu = jnp.pad(p['v_lora_up'][layer - 1], ((0, HPAD - MV_LORA), (0, 0))).astype(BF16)
        in_specs += [row(D_MODEL), row(RC), full(v0), full(vd), full(vu)]
        args += [x, v_first, v0, vd, vu]
    one = jax.ShapeDtypeStruct((n, RC), F32)
    two = jax.ShapeDtypeStruct((2, n, RC), F32)
    return pl.pallas_call(
        functools.partial(_rwkv_prep_kernel, grp, tm, has_vmix),
        grid=(n // tm,),
        in_specs=in_specs,
        out_specs=[row(RC), row(RC), row(RC), row_d, row_d, row_d, row(RC), row(RC)],
        out_shape=[one, one, one, two, two, two, one, one],
        compiler_params=_cparams(("parallel",)),
        name=f"rwkv_prep_l{layer}",
    )(*args)


def _scan_kernel(nseq, *refs):
    ins = [refs[6 * i:6 * i + 6] for i in range(nseq)]
    y_ref, s_ref = refs[6 * nseq:]
    d = pl.program_id(1)
    c = pl.program_id(2)

    @pl.when(c == 0)
    def _():
        s_ref[...] = jnp.zeros(s_ref.shape, F32)

    sgn = jnp.where(d == 0, 1, -1)
    t_i = lax.broadcasted_iota(jnp.int32, (CHUNK, CHUNK), 0)
    t_j = lax.broadcasted_iota(jnp.int32, (CHUNK, CHUNK), 1)
    cum_mask = jnp.where((t_i - t_j) * sgn >= 0, 1.0, 0.0).astype(BF16)

    bi = lax.broadcasted_iota(jnp.int32, (BD, BD), 0)
    bj = lax.broadcasted_iota(jnp.int32, (BD, BD), 1)
    shift = int(math.log2(CHUNK))
    same_head = (bi >> shift) == (bj >> shift)
    lag = ((bi & (CHUNK - 1)) - (bj & (CHUNK - 1))) * sgn
    strict = same_head & (lag > 0)
    incl = same_head & (lag >= 0)
    eye = jnp.where(bi == bj, 1.0, 0.0)

    def bd(x):
        return jnp.where(same_head, jnp.concatenate([x] * GRP, axis=0), 0.0).astype(BF16)

    pre = []
    for r_ref, v_ref, kk_ref, lw_ref, kd_ref, kka_ref in ins:
        lw = lw_ref[...]
        hi = lw.astype(BF16)
        r1 = lw - hi.astype(F32)
        mid = r1.astype(BF16)
        lo = (r1 - mid.astype(F32)).astype(BF16)
        cs = _dot(cum_mask, hi) + _dot(cum_mask, mid) + _dot(cum_mask, lo)
        gin = jnp.exp(-cs)
        pre.append(dict(a=jnp.exp(cs - lw) * kk_ref[...], b=-kka_ref[...] * gin, k=kd_ref[...] * gin,
                        q=jnp.exp(cs) * r_ref[...], v=v_ref[...],
                        g_last=jnp.exp(jnp.sum(lw, axis=0, keepdims=True))))

    chains = [(i, g) for i in range(nseq) for g in range(HEADS // GRP)]
    st = []
    for i, g in chains:
        cols = slice(g * BD, (g + 1) * BD)
        e = {nm: bd(pre[i][nm][:, cols]) for nm in ('a', 'b', 'k', 'q', 'v')}
        e['g_last'] = pre[i]['g_last'][:, cols]
        st.append(e)
    for e in st:
        e['pw'] = jnp.where(strict, _dot_nt(e['a'], e['b']), 0.0)
        e['tinv'] = eye + e['pw']
    for e in st:
        e['l_ak'] = jnp.where(strict, _dot_nt(e['a'], e['k']), 0.0).astype(BF16)
    for e in st:
        e['l_qb'] = jnp.where(incl, _dot_nt(e['q'], e['b']), 0.0).astype(BF16)
    for e in st:
        e['l_qk'] = jnp.where(incl, _dot_nt(e['q'], e['k']), 0.0).astype(BF16)

    for _ in range(int(math.log2(CHUNK)) - 1):
        for e in st:
            pwb = e['pw'].astype(BF16)
            e['pw'] = _dot(pwb, pwb)
            e['tinv'] = e['tinv'] + _dot(e['tinv'].astype(BF16), e['pw'].astype(BF16))

    for n, e in enumerate(st):
        e['s0'] = s_ref[n]
        e['s0b'] = e['s0'].astype(BF16)
        e['rhs'] = _dot_nt(e['a'], e['s0b']) + _dot(e['l_ak'], e['v'])
    for e in st:
        e['ub'] = _dot(e['tinv'].astype(BF16), e['rhs'].astype(BF16)).astype(BF16)
    for (i, g), e in zip(chains, st):
        y = _dot_nt(e['q'], e['s0b']) + _dot(e['l_qb'], e['ub']) + _dot(e['l_qk'], e['v'])
        y_ref[i, :, g * BD:(g + 1) * BD] = (y[0:CHUNK] + y[CHUNK:2 * CHUNK]
                                            + y[2 * CHUNK:3 * CHUNK] + y[3 * CHUNK:4 * CHUNK])
    for n, e in enumerate(st):
        s_ref[n] = (e['s0'] + _dot_tn(e['ub'], e['b']) + _dot_tn(e['v'], e['k'])) * e['g_last']


def _rwkv_scan(r, v, kk, lw2, kd2, kka2, grp, gi, nseq):
    b, t = grp.b[gi], grp.t[gi]
    assert b % nseq == 0
    nc = t // CHUNK
    off = grp.off[gi] // CHUNK

    def blk(bi, d, c, i):
        return (bi * nseq + i) * nc + jnp.where(d == 0, c, nc - 1 - c)

    in_specs, args = [], []
    for i in range(nseq):
        one = pl.BlockSpec((CHUNK, RC), lambda bi, d, c, i=i: (off + blk(bi, d, c, i), 0))
        two = pl.BlockSpec((None, CHUNK, RC), lambda bi, d, c, i=i: (d, off + blk(bi, d, c, i), 0))
        in_specs += [one, one, one, two, two, two]
        args += [r, v, kk, lw2, kd2, kka2]
    y = pl.pallas_call(
        functools.partial(_scan_kernel, nseq),
        grid=(b // nseq, 2, nc),
        in_specs=in_specs,
        out_specs=pl.BlockSpec((None, None, nseq, CHUNK, RC),
                               lambda bi, d, c: (d, bi, 0, jnp.where(d == 0, c, nc - 1 - c), 0)),
        out_shape=jax.ShapeDtypeStruct((2, b // nseq, nseq, t, RC), F32),
        scratch_shapes=[pltpu.VMEM((nseq * (HEADS // GRP), BD, BD), F32)],
        compiler_params=_cparams(("parallel", "arbitrary", "arbitrary")),
        name=f"rwkv_scan_g{gi}",
    )(*args)
    return y.reshape(2, b * t, RC)


def _mix_kernel(alpha, x_ref, att_ref, y_ref, g_ref, bonus_ref, lng_ref, lnb_ref, ones_ref, wg_ref,
                wpa_ref, wpb_ref, wo_ref, g1_ref, b1_ref, o_ref):
    ones_bd = ones_ref[...]
    y = y_ref[0] + y_ref[1]
    mean = _segsum(y, ones_bd) * (1.0 / RH)
    dlt = y - mean
    var = _segsum(dlt * dlt, ones_bd) * (1.0 / RH)
    yn = dlt * lax.rsqrt(var + GN_EPS) * lng_ref[...] + lnb_ref[...]
    rw = ((yn + bonus_ref[...]) * g_ref[...]).astype(BF16)

    x = x_ref[...]
    gates = _dot(x.astype(BF16), wg_ref[...])
    mixed = (_sigmoid(gates[:, :D_MODEL]) * _dot(att_ref[...], wpa_ref[...])
             + _sigmoid(gates[:, D_MODEL:]) * _dot(rw, wpb_ref[...]))
    o_ref[...] = _layernorm(alpha * x + _dot(mixed.astype(BF16), wo_ref[...]), g1_ref[...], b1_ref[...])


def _mix(x, att, y2, g, bonus, p, w_in_l, layer, grp, tm, alpha):
    n = grp.n
    row2 = lambda a: a.reshape(1, -1)
    lng, lnb = row2(p['lnx_g'][layer]), row2(p['lnx_b'][layer])
    ones_bd = _ones_bd()
    wg = w_in_l[:, OFF_GA:].astype(BF16)
    wpa, wpb, wo = (p[nm][layer].astype(BF16) for nm in ('w_pa', 'w_pb', 'w_o'))
    g1, b1 = row2(p['ln1_g'][layer]), row2(p['ln1_b'][layer])
    full = lambda a: pl.BlockSpec(a.shape, lambda i: (0,) * a.ndim)
    row = lambda c: pl.BlockSpec((tm, c), lambda i: (i, 0))
    return pl.pallas_call(
        functools.partial(_mix_kernel, alpha),
        grid=(n // tm,),
        in_specs=[row(D_MODEL), row(RC), pl.BlockSpec((2, tm, RC), lambda i: (0, i, 0)), row(RC), row(RC),
                  full(lng), full(lnb), full(ones_bd), full(wg), full(wpa), full(wpb), full(wo),
                  full(g1), full(b1)],
        out_specs=row(D_MODEL),
        out_shape=jax.ShapeDtypeStruct((n, D_MODEL), F32),
        compiler_params=_cparams(("parallel",)),
        name=f"mix_l{layer}",
    )(x, att, y2, g, bonus, lng, lnb, ones_bd, wg, wpa, wpb, wo, g1, b1)


def _ffn_kernel(grp, tm, alpha, x_ref, xp_ref, xn_ref, p_ref, wup_ref, cw_ref, cb_ref, wdn_ref, wpg_ref,
                wpp_ref, g2_ref, b2_ref, o_ref, ue_ref, hg_ref):
    pos, tlen = grp.seq_pos(pl.program_id(0) * tm)
    keep_prev = (pos != 0).astype(F32)
    keep_next = (pos + tm != tlen).astype(F32)
    x = x_ref[...]
    xb = x.astype(BF16)
    xe = jnp.concatenate([(xp_ref[...] * keep_prev).astype(BF16), xb,
                          (xn_ref[...] * keep_next).astype(BF16)], axis=0)
    acc = alpha * x + _sigmoid(_dot(xb, wpg_ref[...])) * _dot(p_ref[...].astype(BF16), wpp_ref[...])
    ue_ref[0] = _dot(xe, wup_ref[0])
    for j in range(N_FF):
        buf = j % 2
        if j + 1 < N_FF:
            ue_ref[1 - buf] = _dot(xe, wup_ref[j + 1])
        cw = cw_ref[j]
        u = (cw[0:1] * ue_ref[buf, pl.ds(HALO - 1, tm), :] + cw[1:2] * ue_ref[buf, pl.ds(HALO, tm), :]
             + cw[2:3] * ue_ref[buf, pl.ds(HALO + 1, tm), :] + cb_ref[j])
        hg = jax.nn.gelu(u[:, :FF_BLK]) * u[:, FF_BLK:]
        hg_ref[:, j * FF_BLK:(j + 1) * FF_BLK] = hg.astype(BF16)
    acc = acc + _dot(hg_ref[...], wdn_ref[...])
    o_ref[...] = _layernorm(acc, g2_ref[...], b2_ref[...])


def _ffn(x, pe, p, layer, grp, tm, alpha):
    n = grp.n
    row2 = lambda a: a.reshape(1, -1)

    def interleave(a):
        lead = a.shape[:-1]
        a = a.reshape(lead + (2, N_FF, FF_BLK))
        a = jnp.moveaxis(a, -2, 0)
        return a.reshape((N_FF,) + lead + (2 * FF_BLK,))

    wup = interleave(p['w_ffn_up'][layer]).astype(BF16)
    cw = jnp.pad(interleave(p['conv_w'][layer]), ((0, 0), (0, 5), (0, 0)))
    cb = interleave(p['conv_b'][layer].reshape(1, -1))
    wdn = p['w_ffn_down'][layer].astype(BF16)
    wpg = p['w_pe_gate'][layer].astype(BF16)
    wpp = p['w_pe_proj'][layer].astype(BF16)
    g2, b2 = row2(p['ln2_g'][layer]), row2(p['ln2_b'][layer])

    nbh = n // HALO
    full = lambda a: pl.BlockSpec(a.shape, lambda i: (0,) * a.ndim, pipeline_mode=pl.Buffered(1))
    row = lambda c: pl.BlockSpec((tm, c), lambda i: (i, 0))
    return pl.pallas_call(
        functools.partial(_ffn_kernel, grp, tm, alpha),
        grid=(n // tm,),
        in_specs=[row(D_MODEL),
                  pl.BlockSpec((HALO, D_MODEL), lambda i: (jnp.maximum(i * (tm // HALO) - 1, 0), 0)),
                  pl.BlockSpec((HALO, D_MODEL), lambda i: (jnp.minimum((i + 1) * (tm // HALO), nbh - 1), 0)),
                  row(P_DIM), full(wup), full(cw), full(cb), full(wdn), full(wpg), full(wpp), full(g2),
                  full(b2)],
        out_specs=row(D_MODEL),
        out_shape=jax.ShapeDtypeStruct((n, D_MODEL), F32),
        scratch_shapes=[pltpu.VMEM((2, tm + 2 * HALO, 2 * FF_BLK), F32), pltpu.VMEM((tm, D_FF), BF16)],
        compiler_params=_cparams(("parallel",)),
        name=f"ffn_l{layer}",
    )(x, x, x, pe, wup, cw, cb, wdn, wpg, wpp, g2, b2)


def kernel(x_prompt, x_sample, p_prompt, p_sample, w_in, q_norm_g, kv_norm_g, w_uq, w_ukv, tshift_mu, w0, w_lora_up, a0, a_lora_up, g_lora_up, k_k, k_a, r_k, v0, v_lora_down, v_lora_up, lnx_g, lnx_b, w_pa, w_pb, w_o, ln1_g, ln1_b, w_ffn_up, conv_w, conv_b, w_ffn_down, w_pe_gate, w_pe_proj, ln2_g, ln2_b):
    depth = w_in.shape[0]
    alpha = (2 * depth) ** 0.25
    b0, t0, _ = x_prompt.shape
    b1, t1, _ = x_sample.shape
    grp = _Groups(b0, t0, b1, t1)
    n = grp.n
    params = dict(tshift_mu=tshift_mu, w0=w0, w_lora_up=w_lora_up, a0=a0, a_lora_up=a_lora_up,
                  g_lora_up=g_lora_up, k_k=k_k, k_a=k_a, r_k=r_k, v0=v0, v_lora_down=v_lora_down,
                  v_lora_up=v_lora_up, lnx_g=lnx_g, lnx_b=lnx_b, w_pa=w_pa, w_pb=w_pb, w_o=w_o,
                  ln1_g=ln1_g, ln1_b=ln1_b, w_ffn_up=w_ffn_up, conv_w=conv_w, conv_b=conv_b,
                  w_ffn_down=w_ffn_down, w_pe_gate=w_pe_gate, w_pe_proj=w_pe_proj, ln2_g=ln2_g,
                  ln2_b=ln2_b)

    tmin = min(t0, t1)
    tm = min(256, tmin)
    tq = min(1024, tmin)
    tk = min(1024, tmin)
    x = jnp.concatenate([x_prompt.reshape(b0 * t0, D_MODEL), x_sample.reshape(b1 * t1, D_MODEL)], 0)
    pe = jnp.concatenate([p_prompt.reshape(depth, b0 * t0, P_DIM), p_sample.reshape(depth, b1 * t1, P_DIM)], 1)
    tab = _rope_tables(max(t0, t1))

    v_first = None
    for layer in range(depth):
        q, k, v, z = _in_proj(x, tab, w_in[layer], q_norm_g[layer], kv_norm_g[layer], w_uq[layer],
                              w_ukv[layer], grp, tm)
        att = jnp.concatenate([_attention(q, k, v, grp, gi, tq, tk) for gi in range(2)], 0)
        r, vv, kk, lw2, kd2, kka2, g, bonus = _rwkv_prep(z, x, v_first, params, layer, grp, tm)
        if layer == 0:
            v_first = vv
        y2 = jnp.concatenate([_rwkv_scan(r, vv, kk, lw2, kd2, kka2, grp, gi, math.gcd(grp.b[gi], SCAN_SEQS))
                              for gi in range(2)], 1)
        x = _mix(x, att, y2, g, bonus, params, w_in[layer], layer, grp, tm, alpha)
        x = _ffn(x, pe[layer], params, layer, grp, tm, alpha)

    return (x[:grp.n0].reshape(b0, t0, D_MODEL), x[grp.n0:].reshape(b1, t1, D_MODEL))
```

```python
import functools
import math

import jax
import jax.numpy as jnp
from jax import lax
from jax.experimental import pallas as pl
from jax.experimental.pallas import tpu as pltpu

F32 = jnp.float32
BF16 = jnp.bfloat16

D_MODEL = 1024
HEADS = 8
NOPE = 64
ROPE = 32
HALF = ROPE // 2
VDIM = 64
QK = NOPE + ROPE
HPAD = 128
Q_LORA = 768
KV_LORA = 256
ROPE_THETA = 10000.0
RH = 64
RC = HEADS * RH
LORA = 64
GATE_LORA = 128
MV_LORA = 32
DECAY_SCALE = 0.606531
GN_EPS = 64e-5
D_FF = 2816
FF_BLK = 256
N_FF = D_FF // FF_BLK
P_DIM = 256
LN_EPS = 1e-5
RMS_EPS = 1e-6
CHUNK = 64
GRP = 2
BD = GRP * CHUNK
SCAN_SEQS = 4
HALO = 16
ATT_SEGS = 4
ATT_ONES = 16

OFF_CKV = Q_LORA
OFF_KR = OFF_CKV + KV_LORA
OFF_RWKV = OFF_KR + ROPE
RWKV_COLS = 3 * RC + 2 * LORA + 2 * LORA + GATE_LORA
OFF_GA = OFF_RWKV + RWKV_COLS
OFF_GB = OFF_GA + D_MODEL

VMEM_LIMIT = 56 * 1024 * 1024


def _cparams(sem):
    return pltpu.CompilerParams(dimension_semantics=sem, vmem_limit_bytes=VMEM_LIMIT)


def _dot(a, b):
    return jnp.dot(a, b, preferred_element_type=F32)


def _dot_nt(a, b):
    return lax.dot_general(a, b, (((1,), (1,)), ((), ())), preferred_element_type=F32)


def _dot_tn(a, b):
    return lax.dot_general(a, b, (((0,), (0,)), ((), ())), preferred_element_type=F32)


def _split2(x):
    hi = x.astype(BF16)
    lo = (x - hi.astype(F32)).astype(BF16)
    return hi, lo


def _segsum(x, ones_bd):
    hi, lo = _split2(x)
    return _dot(hi, ones_bd) + _dot(lo, ones_bd)


def _sigmoid(x):
    return 1.0 / (1.0 + jnp.exp(-x))


def _layernorm(x, g, b):
    mu = jnp.mean(x, axis=-1, keepdims=True)
    d = x - mu
    var = jnp.mean(d * d, axis=-1, keepdims=True)
    return d * lax.rsqrt(var + LN_EPS) * g + b


class _Groups:
    def __init__(self, b0, t0, b1, t1):
        self.b = (b0, b1)
        self.t = (t0, t1)
        self.n0 = b0 * t0
        self.n = b0 * t0 + b1 * t1
        self.off = (0, self.n0)

    def seq_pos(self, row0):
        in0 = row0 < self.n0
        pos = jnp.where(in0, lax.rem(row0, self.t[0]), lax.rem(row0 - self.n0, self.t[1]))
        tlen = jnp.where(in0, self.t[0], self.t[1])
        return pos, tlen


def _in_proj_kernel(x_ref, tab_ref, w1_ref, wz_ref, qg_ref, kvg_ref, wuq_ref, wuk_ref, wuv_ref,
                    e_ref, q_ref, k_ref, v_ref, z_ref):
    xb = x_ref[...].astype(BF16)
    h1 = _dot(xb, w1_ref[...])
    z_ref[...] = _dot(xb, wz_ref[...])

    cq = h1[:, :Q_LORA]
    cqn = cq * lax.rsqrt(jnp.mean(cq * cq, axis=-1, keepdims=True) + RMS_EPS) * qg_ref[...]
    q = _dot(cqn.astype(BF16), wuq_ref[...])
    cq_tab = tab_ref[:, 0:HPAD]
    sa_tab = tab_ref[:, HPAD:2 * HPAD]
    sb_tab = tab_ref[:, 2 * HPAD:3 * HPAD]
    for h in range(HEADS):
        qh = q[:, h * HPAD:(h + 1) * HPAD]
        out = qh * cq_tab + pltpu.roll(qh, HALF, 1) * sa_tab + pltpu.roll(qh, HPAD - HALF, 1) * sb_tab
        q_ref[:, h * HPAD:(h + 1) * HPAD] = out.astype(BF16)

    ckv = h1[:, OFF_CKV:OFF_KR]
    ckvn = ckv * lax.rsqrt(jnp.mean(ckv * ckv, axis=-1, keepdims=True) + RMS_EPS) * kvg_ref[...]
    ckvb = ckvn.astype(BF16)
    kr = h1[:, OFF_KR:OFF_KR + HPAD]
    krs = h1[:, OFF_KR + HPAD:OFF_KR + 2 * HPAD]
    krot = kr * tab_ref[:, 3 * HPAD:4 * HPAD] + krs * tab_ref[:, 4 * HPAD:5 * HPAD]
    k = _dot(ckvb, wuk_ref[...]) + _dot(krot.astype(BF16), e_ref[...])
    k_ref[...] = k.astype(BF16)
    v_ref[...] = _dot_nt(wuv_ref[...], ckvb).astype(BF16)


def _rope_tables(t_max):
    pos = jnp.arange(t_max, dtype=F32)
    inv_freq = ROPE_THETA ** (-jnp.arange(0, ROPE, 2, dtype=F32) / ROPE)
    ang = pos[:, None] * inv_freq[None, :]
    cos, sin = jnp.cos(ang), jnp.sin(ang)
    scale = QK ** -0.5 * math.log2(math.e)
    zeros = lambda n: jnp.zeros((t_max, n), F32)
    cq = jnp.concatenate([jnp.ones((t_max, NOPE), F32), cos, cos, zeros(HPAD - QK)], 1) * scale
    sa = jnp.concatenate([zeros(NOPE + HALF), sin, zeros(HPAD - QK)], 1) * scale
    sb = jnp.concatenate([zeros(NOPE), -sin, zeros(HALF + HPAD - QK)], 1) * scale
    ck = jnp.concatenate([cos, cos, zeros(HPAD - ROPE)], 1)
    sk = jnp.concatenate([-sin, sin, zeros(HPAD - ROPE)], 1)
    return jnp.concatenate([cq, sa, sb, ck, sk], 1)


def _in_proj(x, tab, w_in_l, q_norm_g, kv_norm_g, w_uq, w_ukv, grp, tm):
    n = grp.n
    w_kr = w_in_l[:, OFF_KR:OFF_RWKV]
    pad = jnp.zeros((D_MODEL, HPAD - ROPE), F32)
    w1 = jnp.concatenate([w_in_l[:, :OFF_KR], w_kr, pad,
                          w_kr[:, HALF:], w_kr[:, :HALF], pad], 1).astype(BF16)
    wz = w_in_l[:, OFF_RWKV:OFF_GA].astype(BF16)
    wuq = jnp.pad(w_uq.reshape(Q_LORA, HEADS, QK), ((0, 0), (0, 0), (0, HPAD - QK)))
    wuq = wuq.reshape(Q_LORA, HEADS * HPAD).astype(BF16)
    wkv = w_ukv.reshape(KV_LORA, HEADS, NOPE + VDIM)
    wuk = jnp.pad(wkv[:, :, :NOPE], ((0, 0), (0, 0), (0, HPAD - NOPE)))
    wuk = wuk.reshape(KV_LORA, HEADS * HPAD).astype(BF16)
    wuv = wkv[:, :, NOPE:].reshape(KV_LORA, HEADS * VDIM).T.astype(BF16)
    place = (jnp.arange(HEADS * HPAD)[None, :] % HPAD) == (NOPE + jnp.arange(HPAD)[:, None])
    place = jnp.where(jnp.arange(HPAD)[:, None] < ROPE, place, False).astype(BF16)

    bt0, bt1 = grp.t[0] // tm, grp.t[1] // tm
    nb0 = grp.n0 // tm

    def tab_map(i):
        return (jnp.where(i < nb0, lax.rem(i, bt0), lax.rem(i - nb0, bt1)), 0)

    full = lambda a: pl.BlockSpec(a.shape, lambda i: (0,) * a.ndim)
    row = lambda c: pl.BlockSpec((tm, c), lambda i: (i, 0))
    qg = q_norm_g.reshape(1, Q_LORA)
    kvg = kv_norm_g.reshape(1, KV_LORA)
    return pl.pallas_call(
        _in_proj_kernel,
        grid=(n // tm,),
        in_specs=[row(D_MODEL), pl.BlockSpec((tm, 5 * HPAD), tab_map), full(w1), full(wz), full(qg),
                  full(kvg), full(wuq), full(wuk), full(wuv), full(place)],
        out_specs=[row(HEADS * HPAD), row(HEADS * HPAD),
                   pl.BlockSpec((HEADS * VDIM, tm), lambda i: (0, i)), row(RWKV_COLS)],
        out_shape=[jax.ShapeDtypeStruct((n, HEADS * HPAD), BF16),
                   jax.ShapeDtypeStruct((n, HEADS * HPAD), BF16),
                   jax.ShapeDtypeStruct((HEADS * VDIM, n), BF16),
                   jax.ShapeDtypeStruct((n, RWKV_COLS), F32)],
        compiler_params=_cparams(("parallel",)),
        name="in_proj",
    )(x, tab, w1, wz, qg, kvg, wuq, wuk, wuv, place)


def _attn_kernel(q_ref, k_ref, vt_ref, o_ref, m_ref, acc_ref):
    ki = pl.program_id(2)

    @pl.when(ki == 0)
    def _():
        m_ref[...] = jnp.full(m_ref.shape, -jnp.inf, F32)
        acc_ref[...] = jnp.zeros(acc_ref.shape, F32)

    tk, tq = k_ref.shape[0], q_ref.shape[0]
    seg = ATT_SEGS if tk % (8 * ATT_SEGS) == 0 else 1

    def colreduce(fn, x):
        part = fn(x.reshape(seg, tk // (8 * seg), 8, tq), axis=1)
        return fn(fn(part, axis=0), axis=0, keepdims=True)

    def scores(h):
        hs = slice(h * HPAD, (h + 1) * HPAD)
        return _dot_nt(k_ref[:, hs], q_ref[:, hs])

    ones_rows = (lax.broadcasted_iota(jnp.int32, (ATT_ONES, tk), 0) == 0).astype(BF16)
    st_next = scores(0)
    for h in range(HEADS):
        st = st_next
        if h + 1 < HEADS:
            st_next = scores(h + 1)
        m_prev = m_ref[h]
        m_new = jnp.maximum(m_prev, colreduce(jnp.max, st))
        alpha = jnp.exp2(m_prev - m_new)
        p = jnp.exp2(st - m_new).astype(BF16)
        m_ref[h] = m_new
        vt1 = jnp.concatenate([vt_ref[h * VDIM:(h + 1) * VDIM, :], ones_rows], axis=0)
        acc_ref[h] = alpha * acc_ref[h] + _dot(vt1, p)

    @pl.when(ki == pl.num_programs(2) - 1)
    def _():
        for h in range(HEADS):
            acc = acc_ref[h]
            o = acc[:VDIM] / acc[VDIM:VDIM + 1]
            o_ref[:, h * VDIM:(h + 1) * VDIM] = o.T.astype(BF16)


def _attention(q, k, vt, grp, gi, tq, tk):
    b, t = grp.b[gi], grp.t[gi]
    nq, nk = t // tq, t // tk
    qoff, koff = grp.off[gi] // tq, grp.off[gi] // tk
    qmap = lambda bi, qi, ki: (qoff + bi * nq + qi, 0)
    kmap = lambda bi, qi, ki: (koff + bi * nk + ki, 0)
    return pl.pallas_call(
        _attn_kernel,
        grid=(b, nq, nk),
        in_specs=[pl.BlockSpec((tq, HEADS * HPAD), qmap), pl.BlockSpec((tk, HEADS * HPAD), kmap),
                  pl.BlockSpec((HEADS * VDIM, tk), lambda bi, qi, ki: (0, koff + bi * nk + ki))],
        out_specs=pl.BlockSpec((tq, HEADS * VDIM), lambda bi, qi, ki: (bi * nq + qi, 0)),
        out_shape=jax.ShapeDtypeStruct((b * t, HEADS * VDIM), BF16),
        scratch_shapes=[pltpu.VMEM((HEADS, 1, tq), F32), pltpu.VMEM((HEADS, VDIM + ATT_ONES, tq), F32)],
        compiler_params=_cparams(("parallel", "parallel", "arbitrary")),
        name=f"attention_g{gi}",
    )(q, k, vt)


def _rwkv_prep_kernel(grp, tm, has_vmix, *refs):
    if has_vmix:
        (z_ref, zp_ref, zn_ref, mu_ref, wup_ref, w0_ref, aup_ref, a0_ref, gup_ref, kk_ref, ka_ref,
         rk_ref, ones_ref, x_ref, vf_ref, v0_ref, vd_ref, vu_ref,
         r_out, v_out, kk_out, lw_out, kd_out, kka_out, g_out, bonus_out) = refs
    else:
        (z_ref, zp_ref, zn_ref, mu_ref, wup_ref, w0_ref, aup_ref, a0_ref, gup_ref, kk_ref, ka_ref,
         rk_ref, ones_ref,
         r_out, v_out, kk_out, lw_out, kd_out, kka_out, g_out, bonus_out) = refs
    i = pl.program_id(0)
    pos, tlen = grp.seq_pos(i * tm)
    first = pos == 0
    last = pos + tm == tlen

    zt = z_ref[...]
    prev_row = jnp.where(first, 0.0, zp_ref[7:8, :])
    next_row = jnp.where(last, 0.0, zn_ref[0:1, :])
    rows = lax.broadcasted_iota(jnp.int32, zt.shape, 0)
    z_prev = jnp.where(rows == 0, prev_row, pltpu.roll(zt, 1, 0))
    z_next = jnp.where(rows == tm - 1, next_row, pltpu.roll(zt, tm - 1, 0))
    zm = zt + mu_ref[...] * (0.5 * (z_prev + z_next) - zt)

    r = zm[:, 0:RC]
    k = zm[:, RC:2 * RC]
    v = zm[:, 2 * RC:3 * RC]
    o = 3 * RC
    zw = zm[:, o:o + 2 * LORA]
    za = zm[:, o + 2 * LORA:o + 4 * LORA]
    zg = zm[:, o + 4 * LORA:o + 4 * LORA + GATE_LORA]

    lw2 = -DECAY_SCALE * _sigmoid(w0_ref[...] + _dot(jnp.tanh(zw).astype(BF16), wup_ref[...]))
    a2 = _sigmoid(a0_ref[...] + _dot(za.astype(BF16), aup_ref[...]))
    g = _dot(_sigmoid(zg).astype(BF16), gup_ref[...])
    if has_vmix:
        xd = _dot(x_ref[...].astype(BF16), vd_ref[...])
        gate = _sigmoid(v0_ref[...] + _dot(xd.astype(BF16), vu_ref[...]))
        v = v + (vf_ref[...] - v) * gate

    ones_bd = ones_ref[...]
    kkr = k * kk_ref[...]
    kk = kkr * lax.rsqrt(_segsum(kkr * kkr, ones_bd) + 1e-12)
    ksum = jnp.zeros_like(k)
    for d in range(2):
        a_d = a2[:, d * RC:(d + 1) * RC]
        kd = k * (1.0 + (a_d - 1.0) * ka_ref[...])
        ksum = ksum + kd
        lw_out[d] = lw2[:, d * RC:(d + 1) * RC]
        kd_out[d] = kd
        kka_out[d] = kk * a_d
    r_out[...] = r
    v_out[...] = v
    kk_out[...] = kk
    g_out[...] = g
    bonus_out[...] = _segsum(r * ksum * rk_ref[...], ones_bd) * v


def _blockdiag2(w):
    z = jnp.zeros_like(w[0])
    return jnp.concatenate([jnp.concatenate([w[0], z], 1), jnp.concatenate([z, w[1]], 1)], 0)


def _ones_bd():
    idx = jnp.arange(RC) // RH
    return (idx[:, None] == idx[None, :]).astype(BF16)


def _rwkv_prep(z, x, v_first, p, layer, grp, tm):
    n = grp.n
    has_vmix = layer > 0
    row2 = lambda a: a.reshape(1, -1)
    mu = row2(p['tshift_mu'][layer])
    wup = _blockdiag2(p['w_lora_up'][layer]).astype(BF16)
    w0 = row2(p['w0'][layer])
    aup = _blockdiag2(p['a_lora_up'][layer]).astype(BF16)
    a0 = row2(p['a0'][layer])
    gup = p['g_lora_up'][layer].astype(BF16)
    kk_p, ka_p, rk_p = row2(p['k_k'][layer]), row2(p['k_a'][layer]), row2(p['r_k'][layer])
    ones_bd = _ones_bd()

    nb8 = n // 8
    full = lambda a: pl.BlockSpec(a.shape, lambda i: (0,) * a.ndim)
    row = lambda c: pl.BlockSpec((tm, c), lambda i: (i, 0))
    row_d = pl.BlockSpec((2, tm, RC), lambda i: (0, i, 0))
    in_specs = [row(RWKV_COLS),
                pl.BlockSpec((8, RWKV_COLS), lambda i: (jnp.maximum(i * (tm // 8) - 1, 0), 0)),
                pl.BlockSpec((8, RWKV_COLS), lambda i: (jnp.minimum((i + 1) * (tm // 8), nb8 - 1), 0)),
                full(mu), full(wup), full(w0), full(aup), full(a0), full(gup), full(kk_p), full(ka_p),
                full(rk_p), full(ones_bd)]
    args = [z, z, z, mu, wup, w0, aup, a0, gup, kk_p, ka_p, rk_p, ones_bd]
    if has_vmix:
        v0 = row2(p['v0'][layer - 1])
        vd = jnp.pad(p['v_lora_down'][layer - 1], ((0, 0), (0, HPAD - MV_LORA))).astype(BF16)
        vu = jnp.pad(p['v_lora_up'][layer - 1], ((0, HPAD - MV_LORA), (0, 0))).astype(BF16)
        in_specs += [row(D_MODEL), row(RC), full(v0), full(vd), full(vu)]
        args += [x, v_first, v0, vd, vu]
    one = jax.ShapeDtypeStruct((n, RC), F32)
    two = jax.ShapeDtypeStruct((2, n, RC), F32)
    return pl.pallas_call(
        functools.partial(_rwkv_prep_kernel, grp, tm, has_vmix),
        grid=(n // tm,),
        in_specs=in_specs,
        out_specs=[row(RC), row(RC), row(RC), row_d, row_d, row_d, row(RC), row(RC)],
        out_shape=[one, one, one, two, two, two, one, one],
        compiler_params=_cparams(("parallel",)),
        name=f"rwkv_prep_l{layer}",
    )(*args)


def _scan_kernel(nseq, *refs):
    ins = [refs[6 * i:6 * i + 6] for i in range(nseq)]
    y_ref, s_ref = refs[6 * nseq:]
    d = pl.program_id(1)
    c = pl.program_id(2)

    @pl.when(c == 0)
    def _():
        s_ref[...] = jnp.zeros(s_ref.shape, F32)

    sgn = jnp.where(d == 0, 1, -1)
    t_i = lax.broadcasted_iota(jnp.int32, (CHUNK, CHUNK), 0)
    t_j = lax.broadcasted_iota(jnp.int32, (CHUNK, CHUNK), 1)
    cum_mask = jnp.where((t_i - t_j) * sgn >= 0, 1.0, 0.0).astype(BF16)

    bi = lax.broadcasted_iota(jnp.int32, (BD, BD), 0)
    bj = lax.broadcasted_iota(jnp.int32, (BD, BD), 1)
    shift = int(math.log2(CHUNK))
    same_head = (bi >> shift) == (bj >> shift)
    lag = ((bi & (CHUNK - 1)) - (bj & (CHUNK - 1))) * sgn
    strict = same_head & (lag > 0)
    incl = same_head & (lag >= 0)
    eye = jnp.where(bi == bj, 1.0, 0.0)

    def bd(x):
        return jnp.where(same_head, jnp.concatenate([x] * GRP, axis=0), 0.0).astype(BF16)

    pre = []
    for r_ref, v_ref, kk_ref, lw_ref, kd_ref, kka_ref in ins:
        lw = lw_ref[...]
        hi = lw.astype(BF16)
        r1 = lw - hi.astype(F32)
        mid = r1.astype(BF16)
        lo = (r1 - mid.astype(F32)).astype(BF16)
        cs = _dot(cum_mask, hi) + _dot(cum_mask, mid) + _dot(cum_mask, lo)
        gin = jnp.exp(-cs)
        pre.append(dict(a=jnp.exp(cs - lw) * kk_ref[...], b=-kka_ref[...] * gin, k=kd_ref[...] * gin,
                        q=jnp.exp(cs) * r_ref[...], v=v_ref[...],
                        g_last=jnp.exp(jnp.sum(lw, axis=0, keepdims=True))))

    chains = [(i, g) for i in range(nseq) for g in range(HEADS // GRP)]
    st = []
    for i, g in chains:
        cols = slice(g * BD, (g + 1) * BD)
        e = {nm: bd(pre[i][nm][:, cols]) for nm in ('a', 'b', 'k', 'q', 'v')}
        e['g_last'] = pre[i]['g_last'][:, cols]
        st.append(e)
    for e in st:
        prod = _dot_nt(jnp.concatenate([e['a'], e['q']], axis=0), jnp.concatenate([e['b'], e['k']], axis=0))
        e['pw'] = jnp.where(strict, prod[:BD, :BD], 0.0)
        e['tinv'] = eye + e['pw']
        e['l_ak'] = jnp.where(strict, prod[:BD, BD:], 0.0).astype(BF16)
        e['l_qb'] = jnp.where(incl, prod[BD:, :BD], 0.0).astype(BF16)
        e['l_qk'] = jnp.where(incl, prod[BD:, BD:], 0.0).astype(BF16)

    for _ in range(int(math.log2(CHUNK)) - 1):
        for e in st:
            pwb = e['pw'].astype(BF16)
            e['pw'] = _dot(pwb, pwb)
            e['tinv'] = e['tinv'] + _dot(e['tinv'].astype(BF16), e['pw'].astype(BF16))

    for n, e in enumerate(st):
        e['s0'] = s_ref[n]
        e['s0t'] = e['s0'].T.astype(BF16)
        e['rhs'] = _dot(jnp.concatenate([e['a'], e['l_ak']], axis=1),
                        jnp.concatenate([e['s0t'], e['v']], axis=0))
    for e in st:
        e['ub'] = _dot(e['tinv'].astype(BF16), e['rhs'].astype(BF16)).astype(BF16)
    for (i, g), e in zip(chains, st):
        y = _dot(jnp.concatenate([e['q'], e['l_qb'], e['l_qk']], axis=1),
                 jnp.concatenate([e['s0t'], e['ub'], e['v']], axis=0))
        y_ref[i, :, g * BD:(g + 1) * BD] = sum(y[h * CHUNK:(h + 1) * CHUNK] for h in range(GRP))
    for n, e in enumerate(st):
        upd = _dot_tn(jnp.concatenate([e['ub'], e['v']], axis=0), jnp.concatenate([e['b'], e['k']], axis=0))
        s_ref[n] = (e['s0'] + upd) * e['g_last']


def _rwkv_scan(r, v, kk, lw2, kd2, kka2, grp, gi, nseq):
    b, t = grp.b[gi], grp.t[gi]
    assert b % nseq == 0
    nc = t // CHUNK
    off = grp.off[gi] // CHUNK

    def blk(bi, d, c, i):
        return (bi * nseq + i) * nc + jnp.where(d == 0, c, nc - 1 - c)

    in_specs, args = [], []
    for i in range(nseq):
        one = pl.BlockSpec((CHUNK, RC), lambda bi, d, c, i=i: (off + blk(bi, d, c, i), 0))
        two = pl.BlockSpec((None, CHUNK, RC), lambda bi, d, c, i=i: (d, off + blk(bi, d, c, i), 0))
        in_specs += [one, one, one, two, two, two]
        args += [r, v, kk, lw2, kd2, kka2]
    y = pl.pallas_call(
        functools.partial(_scan_kernel, nseq),
        grid=(b // nseq, 2, nc),
        in_specs=in_specs,
        out_specs=pl.BlockSpec((None, None, nseq, CHUNK, RC),
                               lambda bi, d, c: (d, bi, 0, jnp.where(d == 0, c, nc - 1 - c), 0)),
        out_shape=jax.ShapeDtypeStruct((2, b // nseq, nseq, t, RC), F32),
        scratch_shapes=[pltpu.VMEM((nseq * (HEADS // GRP), BD, BD), F32)],
        compiler_params=_cparams(("parallel", "arbitrary", "arbitrary")),
        name=f"rwkv_scan_g{gi}",
    )(*args)
    return y.reshape(2, b * t, RC)


def _mix_kernel(alpha, nb0, x_ref, att0_ref, att1_ref, y0_ref, y1_ref, g_ref, bonus_ref, lng_ref, lnb_ref,
                ones_ref, wg_ref, wpa_ref, wpb_ref, wo_ref, g1_ref, b1_ref, o_ref):
    in0 = pl.program_id(0) < nb0
    att = jnp.where(in0, att0_ref[...], att1_ref[...])
    ones_bd = ones_ref[...]
    y = jnp.where(in0, y0_ref[0] + y0_ref[1], y1_ref[0] + y1_ref[1])
    mean = _segsum(y, ones_bd) * (1.0 / RH)
    dlt = y - mean
    var = _segsum(dlt * dlt, ones_bd) * (1.0 / RH)
    yn = dlt * lax.rsqrt(var + GN_EPS) * lng_ref[...] + lnb_ref[...]
    rw = ((yn + bonus_ref[...]) * g_ref[...]).astype(BF16)

    x = x_ref[...]
    gates = _dot(x.astype(BF16), wg_ref[...])
    mixed = (_sigmoid(gates[:, :D_MODEL]) * _dot(att, wpa_ref[...])
             + _sigmoid(gates[:, D_MODEL:]) * _dot(rw, wpb_ref[...]))
    o_ref[...] = _layernorm(alpha * x + _dot(mixed.astype(BF16), wo_ref[...]), g1_ref[...], b1_ref[...])


def _mix(x, att, y2, g, bonus, p, w_in_l, layer, grp, tm, alpha):
    n = grp.n
    nb0, nb1 = grp.n0 // tm, (grp.n - grp.n0) // tm
    idx0 = lambda i: jnp.minimum(i, nb0 - 1)
    idx1 = lambda i: jnp.maximum(i - nb0, 0)
    row2 = lambda a: a.reshape(1, -1)
    lng, lnb = row2(p['lnx_g'][layer]), row2(p['lnx_b'][layer])
    ones_bd = _ones_bd()
    wg = w_in_l[:, OFF_GA:].astype(BF16)
    wpa, wpb, wo = (p[nm][layer].astype(BF16) for nm in ('w_pa', 'w_pb', 'w_o'))
    g1, b1 = row2(p['ln1_g'][layer]), row2(p['ln1_b'][layer])
    full = lambda a: pl.BlockSpec(a.shape, lambda i: (0,) * a.ndim)
    row = lambda c: pl.BlockSpec((tm, c), lambda i: (i, 0))
    return pl.pallas_call(
        functools.partial(_mix_kernel, alpha, nb0),
        grid=(n // tm,),
        in_specs=[row(D_MODEL),
                  pl.BlockSpec((tm, RC), lambda i: (idx0(i), 0)), pl.BlockSpec((tm, RC), lambda i: (idx1(i), 0)),
                  pl.BlockSpec((2, tm, RC), lambda i: (0, idx0(i), 0)),
                  pl.BlockSpec((2, tm, RC), lambda i: (0, idx1(i), 0)),
                  row(RC), row(RC),
                  full(lng), full(lnb), full(ones_bd), full(wg), full(wpa), full(wpb), full(wo),
                  full(g1), full(b1)],
        out_specs=row(D_MODEL),
        out_shape=jax.ShapeDtypeStruct((n, D_MODEL), F32),
        compiler_params=_cparams(("parallel",)),
        name=f"mix_l{layer}",
    )(x, att[0], att[1], y2[0], y2[1], g, bonus, lng, lnb, ones_bd, wg, wpa, wpb, wo, g1, b1)


def _ffn_kernel(grp, tm, alpha, x_ref, xp_ref, xn_ref, p0_ref, p1_ref, wup_ref, cw_ref, cb_ref, wdn_ref,
                wpg_ref, wpp_ref, g2_ref, b2_ref, o_ref, ue_ref, hg_ref):
    pos, tlen = grp.seq_pos(pl.program_id(0) * tm)
    pemb = jnp.where(pl.program_id(0) * tm < grp.n0, p0_ref[...], p1_ref[...])
    keep_prev = (pos != 0).astype(F32)
    keep_next = (pos + tm != tlen).astype(F32)
    x = x_ref[...]
    xb = x.astype(BF16)
    xe = jnp.concatenate([(xp_ref[...] * keep_prev).astype(BF16), xb,
                          (xn_ref[...] * keep_next).astype(BF16)], axis=0)
    acc = alpha * x + _sigmoid(_dot(xb, wpg_ref[...])) * _dot(pemb.astype(BF16), wpp_ref[...])
    ue_ref[0] = _dot(xe, wup_ref[0])
    for j in range(N_FF):
        buf = j % 2
        if j + 1 < N_FF:
            ue_ref[1 - buf] = _dot(xe, wup_ref[j + 1])
        cw = cw_ref[j]
        ue = ue_ref[buf]
        rows = ue.shape[0]
        u_prev = pltpu.roll(ue, 1, 0)[HALO:HALO + tm]
        u_next = pltpu.roll(ue, rows - 1, 0)[HALO:HALO + tm]
        u = cw[0:1] * u_prev + cw[1:2] * ue[HALO:HALO + tm] + cw[2:3] * u_next + cb_ref[j]
        hg = jax.nn.gelu(u[:, :FF_BLK]) * u[:, FF_BLK:]
        hg_ref[:, j * FF_BLK:(j + 1) * FF_BLK] = hg.astype(BF16)
    acc = acc + _dot(hg_ref[...], wdn_ref[...])
    o_ref[...] = _layernorm(acc, g2_ref[...], b2_ref[...])


def _ffn(x, pe, p, layer, grp, tm, alpha):
    n = grp.n
    row2 = lambda a: a.reshape(1, -1)

    def interleave(a):
        lead = a.shape[:-1]
        a = a.reshape(lead + (2, N_FF, FF_BLK))
        a = jnp.moveaxis(a, -2, 0)
        return a.reshape((N_FF,) + lead + (2 * FF_BLK,))

    wup = interleave(p['w_ffn_up'][layer]).astype(BF16)
    cw = jnp.pad(interleave(p['conv_w'][layer]), ((0, 0), (0, 5), (0, 0)))
    cb = interleave(p['conv_b'][layer].reshape(1, -1))
    wdn = p['w_ffn_down'][layer].astype(BF16)
    wpg = p['w_pe_gate'][layer].astype(BF16)
    wpp = p['w_pe_proj'][layer].astype(BF16)
    g2, b2 = row2(p['ln2_g'][layer]), row2(p['ln2_b'][layer])

    nbh = n // HALO
    nb0 = grp.n0 // tm
    pe_spec = lambda idx: pl.BlockSpec((None, tm, P_DIM), lambda i: (layer, idx(i), 0))
    full = lambda a: pl.BlockSpec(a.shape, lambda i: (0,) * a.ndim, pipeline_mode=pl.Buffered(1))
    row = lambda c: pl.BlockSpec((tm, c), lambda i: (i, 0))
    return pl.pallas_call(
        functools.partial(_ffn_kernel, grp, tm, alpha),
        grid=(n // tm,),
        in_specs=[row(D_MODEL),
                  pl.BlockSpec((HALO, D_MODEL), lambda i: (jnp.maximum(i * (tm // HALO) - 1, 0), 0)),
                  pl.BlockSpec((HALO, D_MODEL), lambda i: (jnp.minimum((i + 1) * (tm // HALO), nbh - 1), 0)),
                  pe_spec(lambda i: jnp.minimum(i, nb0 - 1)), pe_spec(lambda i: jnp.maximum(i - nb0, 0)),
                  full(wup), full(cw), full(cb), full(wdn), full(wpg), full(wpp), full(g2), full(b2)],
        out_specs=row(D_MODEL),
        out_shape=jax.ShapeDtypeStruct((n, D_MODEL), F32),
        scratch_shapes=[pltpu.VMEM((2, tm + 2 * HALO, 2 * FF_BLK), F32), pltpu.VMEM((tm, D_FF), BF16)],
        compiler_params=_cparams(("parallel",)),
        name=f"ffn_l{layer}",
    )(x, x, x, pe[0], pe[1], wup, cw, cb, wdn, wpg, wpp, g2, b2)


def kernel(x_prompt, x_sample, p_prompt, p_sample, w_in, q_norm_g, kv_norm_g, w_uq, w_ukv, tshift_mu, w0, w_lora_up, a0, a_lora_up, g_lora_up, k_k, k_a, r_k, v0, v_lora_down, v_lora_up, lnx_g, lnx_b, w_pa, w_pb, w_o, ln1_g, ln1_b, w_ffn_up, conv_w, conv_b, w_ffn_down, w_pe_gate, w_pe_proj, ln2_g, ln2_b):
    depth = w_in.shape[0]
    alpha = (2 * depth) ** 0.25
    b0, t0, _ = x_prompt.shape
    b1, t1, _ = x_sample.shape
    grp = _Groups(b0, t0, b1, t1)
    n = grp.n
    params = dict(tshift_mu=tshift_mu, w0=w0, w_lora_up=w_lora_up, a0=a0, a_lora_up=a_lora_up,
                  g_lora_up=g_lora_up, k_k=k_k, k_a=k_a, r_k=r_k, v0=v0, v_lora_down=v_lora_down,
                  v_lora_up=v_lora_up, lnx_g=lnx_g, lnx_b=lnx_b, w_pa=w_pa, w_pb=w_pb, w_o=w_o,
                  ln1_g=ln1_g, ln1_b=ln1_b, w_ffn_up=w_ffn_up, conv_w=conv_w, conv_b=conv_b,
                  w_ffn_down=w_ffn_down, w_pe_gate=w_pe_gate, w_pe_proj=w_pe_proj, ln2_g=ln2_g,
                  ln2_b=ln2_b)

    tmin = min(t0, t1)
    tm = min(256, tmin)
    tq = min(1024, tmin)
    tk = min(2048, tmin)
    x = jnp.concatenate([x_prompt.reshape(b0 * t0, D_MODEL), x_sample.reshape(b1 * t1, D_MODEL)], 0)
    pe = (p_prompt.reshape(depth, b0 * t0, P_DIM), p_sample.reshape(depth, b1 * t1, P_DIM))
    tab = _rope_tables(max(t0, t1))

    v_first = None
    for layer in range(depth):
        q, k, v, z = _in_proj(x, tab, w_in[layer], q_norm_g[layer], kv_norm_g[layer], w_uq[layer],
                              w_ukv[layer], grp, tm)
        att = [_attention(q, k, v, grp, gi, tq, tk) for gi in range(2)]
        r, vv, kk, lw2, kd2, kka2, g, bonus = _rwkv_prep(z, x, v_first, params, layer, grp, tm)
        if layer == 0:
            v_first = vv
        y2 = [_rwkv_scan(r, vv, kk, lw2, kd2, kka2, grp, gi, math.gcd(grp.b[gi], SCAN_SEQS)) for gi in range(2)]
        x = _mix(x, att, y2, g, bonus, params, w_in[layer], layer, grp, tm, alpha)
        x = _ffn(x, pe, params, layer, grp, tm, alpha)

    return (x[:grp.n0].reshape(b0, t0, D_MODEL), x[grp.n0:].reshape(b1, t1, D_MODEL))
```

```python
import functools
import math

import jax
import jax.numpy as jnp
from jax import lax
from jax.experimental import pallas as pl
from jax.experimental.pallas import tpu as pltpu

F32 = jnp.float32
BF16 = jnp.bfloat16

D_MODEL = 1024
HEADS = 8
NOPE = 64
ROPE = 32
HALF = ROPE // 2
VDIM = 64
QK = NOPE + ROPE
HPAD = 128
Q_LORA = 768
KV_LORA = 256
ROPE_THETA = 10000.0
RH = 64
RC = HEADS * RH
LORA = 64
GATE_LORA = 128
MV_LORA = 32
DECAY_SCALE = 0.606531
GN_EPS = 64e-5
D_FF = 2816
FF_BLK = 256
N_FF = D_FF // FF_BLK
P_DIM = 256
LN_EPS = 1e-5
RMS_EPS = 1e-6
CHUNK = 64
GRP = 2
BD = GRP * CHUNK
SCAN_SEQS = 8
HALO = 16
ATT_SEGS = 4
ATT_ONES = 16

OFF_CKV = Q_LORA
OFF_KR = OFF_CKV + KV_LORA
OFF_RWKV = OFF_KR + ROPE
RWKV_COLS = 3 * RC + 2 * LORA + 2 * LORA + GATE_LORA
OFF_GA = OFF_RWKV + RWKV_COLS
OFF_GB = OFF_GA + D_MODEL

VMEM_LIMIT = 56 * 1024 * 1024


def _cparams(sem):
    return pltpu.CompilerParams(dimension_semantics=sem, vmem_limit_bytes=VMEM_LIMIT)


def _dot(a, b):
    return jnp.dot(a, b, preferred_element_type=F32)


def _dot_nt(a, b):
    return lax.dot_general(a, b, (((1,), (1,)), ((), ())), preferred_element_type=F32)


def _dot_tn(a, b):
    return lax.dot_general(a, b, (((0,), (0,)), ((), ())), preferred_element_type=F32)


def _split2(x):
    hi = x.astype(BF16)
    lo = (x - hi.astype(F32)).astype(BF16)
    return hi, lo


def _segsum(x, ones_bd):
    hi, lo = _split2(x)
    return _dot(hi, ones_bd) + _dot(lo, ones_bd)


def _sigmoid(x):
    return 1.0 / (1.0 + jnp.exp(-x))


def _layernorm(x, g, b):
    mu = jnp.mean(x, axis=-1, keepdims=True)
    d = x - mu
    var = jnp.mean(d * d, axis=-1, keepdims=True)
    return d * lax.rsqrt(var + LN_EPS) * g + b


class _Groups:
    def __init__(self, b0, t0, b1, t1):
        self.b = (b0, b1)
        self.t = (t0, t1)
        self.n0 = b0 * t0
        self.n = b0 * t0 + b1 * t1
        self.off = (0, self.n0)

    def seq_pos(self, row0):
        in0 = row0 < self.n0
        pos = jnp.where(in0, lax.rem(row0, self.t[0]), lax.rem(row0 - self.n0, self.t[1]))
        tlen = jnp.where(in0, self.t[0], self.t[1])
        return pos, tlen


def _mla_part(xb, tab_ref, w1_ref, qg_ref, kvg_ref, wuq_ref, wuk_ref, wuv_ref, e_ref, q_ref, k_ref, v_ref):
    h1 = _dot(xb, w1_ref[...])
    cq = h1[:, :Q_LORA]
    cqn = cq * lax.rsqrt(jnp.mean(cq * cq, axis=-1, keepdims=True) + RMS_EPS) * qg_ref[...]
    q = _dot(cqn.astype(BF16), wuq_ref[...])
    cq_tab = tab_ref[:, 0:HPAD]
    sa_tab = tab_ref[:, HPAD:2 * HPAD]
    sb_tab = tab_ref[:, 2 * HPAD:3 * HPAD]
    for h in range(HEADS):
        qh = q[:, h * HPAD:(h + 1) * HPAD]
        out = qh * cq_tab + pltpu.roll(qh, HALF, 1) * sa_tab + pltpu.roll(qh, HPAD - HALF, 1) * sb_tab
        q_ref[:, h * HPAD:(h + 1) * HPAD] = out.astype(BF16)

    ckv = h1[:, OFF_CKV:OFF_KR]
    ckvn = ckv * lax.rsqrt(jnp.mean(ckv * ckv, axis=-1, keepdims=True) + RMS_EPS) * kvg_ref[...]
    ckvb = ckvn.astype(BF16)
    kr = h1[:, OFF_KR:OFF_KR + HPAD]
    krs = h1[:, OFF_KR + HPAD:OFF_KR + 2 * HPAD]
    krot = kr * tab_ref[:, 3 * HPAD:4 * HPAD] + krs * tab_ref[:, 4 * HPAD:5 * HPAD]
    k = _dot(ckvb, wuk_ref[...]) + _dot(krot.astype(BF16), e_ref[...])
    k_ref[...] = k.astype(BF16)
    v_ref[...] = _dot_nt(wuv_ref[...], ckvb).astype(BF16)


def _rope_tables(t_max):
    pos = jnp.arange(t_max, dtype=F32)
    inv_freq = ROPE_THETA ** (-jnp.arange(0, ROPE, 2, dtype=F32) / ROPE)
    ang = pos[:, None] * inv_freq[None, :]
    cos, sin = jnp.cos(ang), jnp.sin(ang)
    scale = QK ** -0.5 * math.log2(math.e)
    zeros = lambda n: jnp.zeros((t_max, n), F32)
    cq = jnp.concatenate([jnp.ones((t_max, NOPE), F32), cos, cos, zeros(HPAD - QK)], 1) * scale
    sa = jnp.concatenate([zeros(NOPE + HALF), sin, zeros(HPAD - QK)], 1) * scale
    sb = jnp.concatenate([zeros(NOPE), -sin, zeros(HALF + HPAD - QK)], 1) * scale
    ck = jnp.concatenate([cos, cos, zeros(HPAD - ROPE)], 1)
    sk = jnp.concatenate([-sin, sin, zeros(HPAD - ROPE)], 1)
    return jnp.concatenate([cq, sa, sb, ck, sk], 1)


def _mla_weights(w_in_l, q_norm_g, kv_norm_g, w_uq, w_ukv):
    w_kr = w_in_l[:, OFF_KR:OFF_RWKV]
    pad = jnp.zeros((D_MODEL, HPAD - ROPE), F32)
    w1 = jnp.concatenate([w_in_l[:, :OFF_KR], w_kr, pad,
                          w_kr[:, HALF:], w_kr[:, :HALF], pad], 1).astype(BF16)
    wz = w_in_l[:, OFF_RWKV:OFF_GA].astype(BF16)
    wuq = jnp.pad(w_uq.reshape(Q_LORA, HEADS, QK), ((0, 0), (0, 0), (0, HPAD - QK)))
    wuq = wuq.reshape(Q_LORA, HEADS * HPAD).astype(BF16)
    wkv = w_ukv.reshape(KV_LORA, HEADS, NOPE + VDIM)
    wuk = jnp.pad(wkv[:, :, :NOPE], ((0, 0), (0, 0), (0, HPAD - NOPE)))
    wuk = wuk.reshape(KV_LORA, HEADS * HPAD).astype(BF16)
    wuv = wkv[:, :, NOPE:].reshape(KV_LORA, HEADS * VDIM).T.astype(BF16)
    place = (jnp.arange(HEADS * HPAD)[None, :] % HPAD) == (NOPE + jnp.arange(HPAD)[:, None])
    place = jnp.where(jnp.arange(HPAD)[:, None] < ROPE, place, False).astype(BF16)
    qg = q_norm_g.reshape(1, Q_LORA)
    kvg = kv_norm_g.reshape(1, KV_LORA)
    return wz, [w1, qg, kvg, wuq, wuk, wuv, place]


def _attn_kernel(q_ref, k_ref, vt_ref, o_ref, m_ref, acc_ref):
    ki = pl.program_id(2)

    @pl.when(ki == 0)
    def _():
        m_ref[...] = jnp.full(m_ref.shape, -jnp.inf, F32)
        acc_ref[...] = jnp.zeros(acc_ref.shape, F32)

    tk, tq = k_ref.shape[0], q_ref.shape[0]
    seg = ATT_SEGS if tk % (8 * ATT_SEGS) == 0 else 1

    def colreduce(fn, x):
        part = fn(x.reshape(seg, tk // (8 * seg), 8, tq), axis=1)
        return fn(fn(part, axis=0), axis=0, keepdims=True)

    def scores(h):
        hs = slice(h * HPAD, (h + 1) * HPAD)
        return _dot_nt(k_ref[:, hs], q_ref[:, hs])

    ones_rows = (lax.broadcasted_iota(jnp.int32, (ATT_ONES, tk), 0) == 0).astype(BF16)
    st_next = scores(0)
    for h in range(HEADS):
        st = st_next
        if h + 1 < HEADS:
            st_next = scores(h + 1)
        m_prev = m_ref[h]
        m_new = jnp.maximum(m_prev, colreduce(jnp.max, st))
        alpha = jnp.exp2(m_prev - m_new)
        p = jnp.exp2(st - m_new).astype(BF16)
        m_ref[h] = m_new
        vt1 = jnp.concatenate([vt_ref[h * VDIM:(h + 1) * VDIM, :], ones_rows], axis=0)
        acc_ref[h] = alpha * acc_ref[h] + _dot(vt1, p)

    @pl.when(ki == pl.num_programs(2) - 1)
    def _():
        for h in range(HEADS):
            acc = acc_ref[h]
            o = acc[:VDIM] / acc[VDIM:VDIM + 1]
            o_ref[:, h * VDIM:(h + 1) * VDIM] = o.T.astype(BF16)


def _attention(q, k, vt, grp, gi, tq, tk):
    b, t = grp.b[gi], grp.t[gi]
    nq, nk = t // tq, t // tk
    qoff, koff = grp.off[gi] // tq, grp.off[gi] // tk
    qmap = lambda bi, qi, ki: (qoff + bi * nq + qi, 0)
    kmap = lambda bi, qi, ki: (koff + bi * nk + ki, 0)
    return pl.pallas_call(
        _attn_kernel,
        grid=(b, nq, nk),
        in_specs=[pl.BlockSpec((tq, HEADS * HPAD), qmap), pl.BlockSpec((tk, HEADS * HPAD), kmap),
                  pl.BlockSpec((HEADS * VDIM, tk), lambda bi, qi, ki: (0, koff + bi * nk + ki))],
        out_specs=pl.BlockSpec((tq, HEADS * VDIM), lambda bi, qi, ki: (bi * nq + qi, 0)),
        out_shape=jax.ShapeDtypeStruct((b * t, HEADS * VDIM), BF16),
        scratch_shapes=[pltpu.VMEM((HEADS, 1, tq), F32), pltpu.VMEM((HEADS, VDIM + ATT_ONES, tq), F32)],
        compiler_params=_cparams(("parallel", "parallel", "arbitrary")),
        name=f"attention_g{gi}",
    )(q, k, vt)


def _front_kernel(grp, tm, has_vmix, *refs):
    (x_ref, xp_ref, xn_ref, tab_ref, wz_ref, w1_ref, qg_ref, kvg_ref, wuq_ref, wuk_ref, wuv_ref, e_ref,
     mu_ref, wup_ref, w0_ref, aup_ref, a0_ref, gup_ref, kk_ref, ka_ref, rk_ref, ones_ref) = refs[:22]
    if has_vmix:
        vf_ref, v0_ref, vd_ref, vu_ref = refs[22:26]
    (q_ref, k_ref, v_ref,
     r_out, v_out, kk_out, lw_out, kd_out, kka_out, g_out, bonus_out) = refs[-11:]
    xb = x_ref[...].astype(BF16)
    zt = _dot(xb, wz_ref[...])
    halo = jnp.concatenate([xp_ref[...], xn_ref[...]], axis=0).astype(BF16)
    zh = _dot(halo, wz_ref[...])
    _mla_part(xb, tab_ref, w1_ref, qg_ref, kvg_ref, wuq_ref, wuk_ref, wuv_ref, e_ref, q_ref, k_ref, v_ref)

    pos, tlen = grp.seq_pos(pl.program_id(0) * tm)
    first = pos == 0
    last = pos + tm == tlen
    prev_row = jnp.where(first, 0.0, zh[7:8, :])
    next_row = jnp.where(last, 0.0, zh[8:9, :])
    rows = lax.broadcasted_iota(jnp.int32, zt.shape, 0)
    z_prev = jnp.where(rows == 0, prev_row, pltpu.roll(zt, 1, 0))
    z_next = jnp.where(rows == tm - 1, next_row, pltpu.roll(zt, tm - 1, 0))
    zm = zt + mu_ref[...] * (0.5 * (z_prev + z_next) - zt)

    r = zm[:, 0:RC]
    k = zm[:, RC:2 * RC]
    v = zm[:, 2 * RC:3 * RC]
    o = 3 * RC
    zw = zm[:, o:o + 2 * LORA]
    za = zm[:, o + 2 * LORA:o + 4 * LORA]
    zg = zm[:, o + 4 * LORA:o + 4 * LORA + GATE_LORA]

    lw2 = -DECAY_SCALE * _sigmoid(w0_ref[...] + _dot(jnp.tanh(zw).astype(BF16), wup_ref[...]))
    a2 = _sigmoid(a0_ref[...] + _dot(za.astype(BF16), aup_ref[...]))
    g = _dot(_sigmoid(zg).astype(BF16), gup_ref[...])
    if has_vmix:
        xd = _dot(xb, vd_ref[...])
        gate = _sigmoid(v0_ref[...] + _dot(xd.astype(BF16), vu_ref[...]))
        v = v + (vf_ref[...] - v) * gate

    ones_bd = ones_ref[...]
    kkr = k * kk_ref[...]
    kk = kkr * lax.rsqrt(_segsum(kkr * kkr, ones_bd) + 1e-12)
    ksum = jnp.zeros_like(k)
    for d in range(2):
        a_d = a2[:, d * RC:(d + 1) * RC]
        kd = k * (1.0 + (a_d - 1.0) * ka_ref[...])
        ksum = ksum + kd
        lw_out[d] = lw2[:, d * RC:(d + 1) * RC]
        kd_out[d] = kd
        kka_out[d] = kk * a_d
    r_out[...] = r
    v_out[...] = v
    kk_out[...] = kk
    g_out[...] = g
    bonus_out[...] = _segsum(r * ksum * rk_ref[...], ones_bd) * v


def _blockdiag2(w):
    z = jnp.zeros_like(w[0])
    return jnp.concatenate([jnp.concatenate([w[0], z], 1), jnp.concatenate([z, w[1]], 1)], 0)


def _ones_bd():
    idx = jnp.arange(RC) // RH
    return (idx[:, None] == idx[None, :]).astype(BF16)


def _front(x, tab, v_first, w_in_l, q_norm_g, kv_norm_g, w_uq, w_ukv, p, layer, grp, tm):
    n = grp.n
    has_vmix = layer > 0
    wz, mla_w = _mla_weights(w_in_l, q_norm_g, kv_norm_g, w_uq, w_ukv)
    row2 = lambda a: a.reshape(1, -1)
    mu = row2(p['tshift_mu'][layer])
    wup = _blockdiag2(p['w_lora_up'][layer]).astype(BF16)
    w0 = row2(p['w0'][layer])
    aup = _blockdiag2(p['a_lora_up'][layer]).astype(BF16)
    a0 = row2(p['a0'][layer])
    gup = p['g_lora_up'][layer].astype(BF16)
    kk_p, ka_p, rk_p = row2(p['k_k'][layer]), row2(p['k_a'][layer]), row2(p['r_k'][layer])
    ones_bd = _ones_bd()

    nb8 = n // 8
    bt0, bt1 = grp.t[0] // tm, grp.t[1] // tm
    nb0 = grp.n0 // tm

    def tab_map(i):
        return (jnp.where(i < nb0, lax.rem(i, bt0), lax.rem(i - nb0, bt1)), 0)

    full = lambda a: pl.BlockSpec(a.shape, lambda i: (0,) * a.ndim, pipeline_mode=pl.Buffered(1))
    row = lambda c: pl.BlockSpec((tm, c), lambda i: (i, 0))
    row_d = pl.BlockSpec((2, tm, RC), lambda i: (0, i, 0))
    weights = [wz] + mla_w + [mu, wup, w0, aup, a0, gup, kk_p, ka_p, rk_p, ones_bd]
    in_specs = [row(D_MODEL),
                pl.BlockSpec((8, D_MODEL), lambda i: (jnp.maximum(i * (tm // 8) - 1, 0), 0)),
                pl.BlockSpec((8, D_MODEL), lambda i: (jnp.minimum((i + 1) * (tm // 8), nb8 - 1), 0)),
                pl.BlockSpec((tm, 5 * HPAD), tab_map)] + [full(w) for w in weights]
    args = [x, x, x, tab] + weights
    if has_vmix:
        v0 = row2(p['v0'][layer - 1])
        vd = jnp.pad(p['v_lora_down'][layer - 1], ((0, 0), (0, HPAD - MV_LORA))).astype(BF16)
        vu = jnp.pad(p['v_lora_up'][layer - 1], ((0, HPAD - MV_LORA), (0, 0))).astype(BF16)
        in_specs += [row(RC), full(v0), full(vd), full(vu)]
        args += [v_first, v0, vd, vu]
    qk = jax.ShapeDtypeStruct((n, HEADS * HPAD), BF16)
    one = jax.ShapeDtypeStruct((n, RC), F32)
    two = jax.ShapeDtypeStruct((2, n, RC), F32)
    return pl.pallas_call(
        functools.partial(_front_kernel, grp, tm, has_vmix),
        grid=(n // tm,),
        in_specs=in_specs,
        out_specs=[row(HEADS * HPAD), row(HEADS * HPAD), pl.BlockSpec((HEADS * VDIM, tm), lambda i: (0, i)),
                   row(RC), row(RC), row(RC), row_d, row_d, row_d, row(RC), row(RC)],
        out_shape=[qk, qk, jax.ShapeDtypeStruct((HEADS * VDIM, n), BF16),
                   one, one, one, two, two, two, one, one],
        compiler_params=_cparams(("parallel",)),
        name=f"front_l{layer}",
    )(*args)


def _scan_kernel(nseq, *refs):
    ins = [refs[6 * i:6 * i + 6] for i in range(nseq)]
    y_ref, s_ref = refs[6 * nseq:]
    d = pl.program_id(1)
    c = pl.program_id(2)

    @pl.when(c == 0)
    def _():
        s_ref[...] = jnp.zeros(s_ref.shape, F32)

    sgn = jnp.where(d == 0, 1, -1)
    t_i = lax.broadcasted_iota(jnp.int32, (CHUNK, CHUNK), 0)
    t_j = lax.broadcasted_iota(jnp.int32, (CHUNK, CHUNK), 1)
    cum_mask = jnp.where((t_i - t_j) * sgn >= 0, 1.0, 0.0).astype(BF16)

    bi = lax.broadcasted_iota(jnp.int32, (BD, BD), 0)
    bj = lax.broadcasted_iota(jnp.int32, (BD, BD), 1)
    shift = int(math.log2(CHUNK))
    same_head = (bi >> shift) == (bj >> shift)
    lag = ((bi & (CHUNK - 1)) - (bj & (CHUNK - 1))) * sgn
    strict = same_head & (lag > 0)
    incl = same_head & (lag >= 0)
    eye = jnp.where(bi == bj, 1.0, 0.0)

    def bd(x):
        return jnp.where(same_head, jnp.concatenate([x] * GRP, axis=0), 0.0).astype(BF16)

    pre = []
    for r_ref, v_ref, kk_ref, lw_ref, kd_ref, kka_ref in ins:
        lw = lw_ref[...]
        hi = lw.astype(BF16)
        r1 = lw - hi.astype(F32)
        mid = r1.astype(BF16)
        lo = (r1 - mid.astype(F32)).astype(BF16)
        cs = _dot(cum_mask, hi) + _dot(cum_mask, mid) + _dot(cum_mask, lo)
        gin = jnp.exp(-cs)
        pre.append(dict(a=jnp.exp(cs - lw) * kk_ref[...], b=-kka_ref[...] * gin, k=kd_ref[...] * gin,
                        q=jnp.exp(cs) * r_ref[...], v=v_ref[...],
                        g_last=jnp.exp(jnp.sum(lw, axis=0, keepdims=True))))

    chains = [(i, g) for i in range(nseq) for g in range(HEADS // GRP)]
    st = []
    for i, g in chains:
        cols = slice(g * BD, (g + 1) * BD)
        e = {nm: bd(pre[i][nm][:, cols]) for nm in ('a', 'b', 'k', 'q', 'v')}
        e['g_last'] = pre[i]['g_last'][:, cols]
        st.append(e)
    for e in st:
        prod = _dot_nt(jnp.concatenate([e['a'], e['q']], axis=0), jnp.concatenate([e['b'], e['k']], axis=0))
        e['pw'] = jnp.where(strict, prod[:BD, :BD], 0.0)
        e['tinv'] = eye + e['pw']
        e['l_ak'] = jnp.where(strict, prod[:BD, BD:], 0.0).astype(BF16)
        e['l_qb'] = jnp.where(incl, prod[BD:, :BD], 0.0).astype(BF16)
        e['l_qk'] = jnp.where(incl, prod[BD:, BD:], 0.0).astype(BF16)

    for e in st:
        pwb = e['pw'].astype(BF16)
        e['pw'] = _dot(pwb, pwb)
    for _ in range(int(math.log2(CHUNK)) - 2):
        for e in st:
            pwb = e['pw'].astype(BF16)
            both = _dot(pwb, jnp.concatenate([pwb, e['tinv'].astype(BF16)], axis=1))
            e['pw'] = both[:, :BD]
            e['tinv'] = e['tinv'] + both[:, BD:]
    for e in st:
        e['tinv'] = e['tinv'] + _dot(e['pw'].astype(BF16), e['tinv'].astype(BF16))

    for n, e in enumerate(st):
        e['s0'] = s_ref[n]
        e['s0t'] = e['s0'].T.astype(BF16)
        e['rhs'] = _dot(jnp.concatenate([e['a'], e['l_ak']], axis=1),
                        jnp.concatenate([e['s0t'], e['v']], axis=0))
    for e in st:
        e['ub'] = _dot(e['tinv'].astype(BF16), e['rhs'].astype(BF16)).astype(BF16)
    for (i, g), e in zip(chains, st):
        y = _dot(jnp.concatenate([e['q'], e['l_qb'], e['l_qk']], axis=1),
                 jnp.concatenate([e['s0t'], e['ub'], e['v']], axis=0))
        y_ref[i, :, g * BD:(g + 1) * BD] = sum(y[h * CHUNK:(h + 1) * CHUNK] for h in range(GRP))
    for n, e in enumerate(st):
        upd = _dot_tn(jnp.concatenate([e['ub'], e['v']], axis=0), jnp.concatenate([e['b'], e['k']], axis=0))
        s_ref[n] = (e['s0'] + upd) * e['g_last']


def _rwkv_scan(r, v, kk, lw2, kd2, kka2, grp, gi, nseq):
    b, t = grp.b[gi], grp.t[gi]
    assert b % nseq == 0
    nc = t // CHUNK
    off = grp.off[gi] // CHUNK

    def blk(bi, d, c, i):
        return (bi * nseq + i) * nc + jnp.where(d == 0, c, nc - 1 - c)

    in_specs, args = [], []
    for i in range(nseq):
        one = pl.BlockSpec((CHUNK, RC), lambda bi, d, c, i=i: (off + blk(bi, d, c, i), 0))
        two = pl.BlockSpec((None, CHUNK, RC), lambda bi, d, c, i=i: (d, off + blk(bi, d, c, i), 0))
        in_specs += [one, one, one, two, two, two]
        args += [r, v, kk, lw2, kd2, kka2]
    y = pl.pallas_call(
        functools.partial(_scan_kernel, nseq),
        grid=(b // nseq, 2, nc),
        in_specs=in_specs,
        out_specs=pl.BlockSpec((None, None, nseq, CHUNK, RC),
                               lambda bi, d, c: (d, bi, 0, jnp.where(d == 0, c, nc - 1 - c), 0)),
        out_shape=jax.ShapeDtypeStruct((2, b // nseq, nseq, t, RC), F32),
        scratch_shapes=[pltpu.VMEM((nseq * (HEADS // GRP), BD, BD), F32)],
        compiler_params=_cparams(("parallel", "arbitrary", "arbitrary")),
        name=f"rwkv_scan_g{gi}",
    )(*args)
    return y.reshape(2, b * t, RC)


def _mix_kernel(alpha, nb0, x_ref, att0_ref, att1_ref, y0_ref, y1_ref, g_ref, bonus_ref, lng_ref, lnb_ref,
                ones_ref, wg_ref, wpa_ref, wpb_ref, wo_ref, g1_ref, b1_ref, o_ref):
    in0 = pl.program_id(0) < nb0
    att = jnp.where(in0, att0_ref[...], att1_ref[...])
    ones_bd = ones_ref[...]
    y = jnp.where(in0, y0_ref[0] + y0_ref[1], y1_ref[0] + y1_ref[1])
    mean = _segsum(y, ones_bd) * (1.0 / RH)
    dlt = y - mean
    var = _segsum(dlt * dlt, ones_bd) * (1.0 / RH)
    yn = dlt * lax.rsqrt(var + GN_EPS) * lng_ref[...] + lnb_ref[...]
    rw = ((yn + bonus_ref[...]) * g_ref[...]).astype(BF16)

    x = x_ref[...]
    gates = _dot(x.astype(BF16), wg_ref[...])
    mixed = (_sigmoid(gates[:, :D_MODEL]) * _dot(att, wpa_ref[...])
             + _sigmoid(gates[:, D_MODEL:]) * _dot(rw, wpb_ref[...]))
    o_ref[...] = _layernorm(alpha * x + _dot(mixed.astype(BF16), wo_ref[...]), g1_ref[...], b1_ref[...])


def _mix(x, att, y2, g, bonus, p, w_in_l, layer, grp, tm, alpha):
    n = grp.n
    nb0, nb1 = grp.n0 // tm, (grp.n - grp.n0) // tm
    idx0 = lambda i: jnp.minimum(i, nb0 - 1)
    idx1 = lambda i: jnp.maximum(i - nb0, 0)
    row2 = lambda a: a.reshape(1, -1)
    lng, lnb = row2(p['lnx_g'][layer]), row2(p['lnx_b'][layer])
    ones_bd = _ones_bd()
    wg = w_in_l[:, OFF_GA:].astype(BF16)
    wpa, wpb, wo = (p[nm][layer].astype(BF16) for nm in ('w_pa', 'w_pb', 'w_o'))
    g1, b1 = row2(p['ln1_g'][layer]), row2(p['ln1_b'][layer])
    full = lambda a: pl.BlockSpec(a.shape, lambda i: (0,) * a.ndim)
    row = lambda c: pl.BlockSpec((tm, c), lambda i: (i, 0))
    return pl.pallas_call(
        functools.partial(_mix_kernel, alpha, nb0),
        grid=(n // tm,),
        in_specs=[row(D_MODEL),
                  pl.BlockSpec((tm, RC), lambda i: (idx0(i), 0)), pl.BlockSpec((tm, RC), lambda i: (idx1(i), 0)),
                  pl.BlockSpec((2, tm, RC), lambda i: (0, idx0(i), 0)),
                  pl.BlockSpec((2, tm, RC), lambda i: (0, idx1(i), 0)),
                  row(RC), row(RC),
                  full(lng), full(lnb), full(ones_bd), full(wg), full(wpa), full(wpb), full(wo),
                  full(g1), full(b1)],
        out_specs=row(D_MODEL),
        out_shape=jax.ShapeDtypeStruct((n, D_MODEL), F32),
        compiler_params=_cparams(("parallel",)),
        name=f"mix_l{layer}",
    )(x, att[0], att[1], y2[0], y2[1], g, bonus, lng, lnb, ones_bd, wg, wpa, wpb, wo, g1, b1)


def _ffn_kernel(grp, tm, alpha, x_ref, xp_ref, xn_ref, p0_ref, p1_ref, wup_ref, cw_ref, cb_ref, wdn_ref,
                wpg_ref, wpp_ref, g2_ref, b2_ref, o_ref, ue_ref, hg_ref):
    pos, tlen = grp.seq_pos(pl.program_id(0) * tm)
    pemb = jnp.where(pl.program_id(0) * tm < grp.n0, p0_ref[...], p1_ref[...])
    keep_prev = (pos != 0).astype(F32)
    keep_next = (pos + tm != tlen).astype(F32)
    x = x_ref[...]
    xb = x.astype(BF16)
    xe = jnp.concatenate([(xp_ref[...] * keep_prev).astype(BF16), xb,
                          (xn_ref[...] * keep_next).astype(BF16)], axis=0)
    acc = alpha * x + _sigmoid(_dot(xb, wpg_ref[...])) * _dot(pemb.astype(BF16), wpp_ref[...])
    ue_ref[0] = _dot(xe, wup_ref[0])
    for j in range(N_FF):
        buf = j % 2
        if j + 1 < N_FF:
            ue_ref[1 - buf] = _dot(xe, wup_ref[j + 1])
        cw = cw_ref[j]
        ue = ue_ref[buf]
        rows = ue.shape[0]
        u_prev = pltpu.roll(ue, 1, 0)[HALO:HALO + tm]
        u_next = pltpu.roll(ue, rows - 1, 0)[HALO:HALO + tm]
        u = cw[0:1] * u_prev + cw[1:2] * ue[HALO:HALO + tm] + cw[2:3] * u_next + cb_ref[j]
        hg = jax.nn.gelu(u[:, :FF_BLK]) * u[:, FF_BLK:]
        hg_ref[:, j * FF_BLK:(j + 1) * FF_BLK] = hg.astype(BF16)
    acc = acc + _dot(hg_ref[...], wdn_ref[...])
    o_ref[...] = _layernorm(acc, g2_ref[...], b2_ref[...])


def _ffn(x, pe, p, layer, grp, tm, alpha):
    n = grp.n
    row2 = lambda a: a.reshape(1, -1)

    def interleave(a):
        lead = a.shape[:-1]
        a = a.reshape(lead + (2, N_FF, FF_BLK))
        a = jnp.moveaxis(a, -2, 0)
        return a.reshape((N_FF,) + lead + (2 * FF_BLK,))

    wup = interleave(p['w_ffn_up'][layer]).astype(BF16)
    cw = jnp.pad(interleave(p['conv_w'][layer]), ((0, 0), (0, 5), (0, 0)))
    cb = interleave(p['conv_b'][layer].reshape(1, -1))
    wdn = p['w_ffn_down'][layer].astype(BF16)
    wpg = p['w_pe_gate'][layer].astype(BF16)
    wpp = p['w_pe_proj'][layer].astype(BF16)
    g2, b2 = row2(p['ln2_g'][layer]), row2(p['ln2_b'][layer])

    nbh = n // HALO
    nb0 = grp.n0 // tm
    pe_spec = lambda idx: pl.BlockSpec((None, tm, P_DIM), lambda i: (layer, idx(i), 0))
    full = lambda a: pl.BlockSpec(a.shape, lambda i: (0,) * a.ndim, pipeline_mode=pl.Buffered(1))
    row = lambda c: pl.BlockSpec((tm, c), lambda i: (i, 0))
    return pl.pallas_call(
        functools.partial(_ffn_kernel, grp, tm, alpha),
        grid=(n // tm,),
        in_specs=[row(D_MODEL),
                  pl.BlockSpec((HALO, D_MODEL), lambda i: (jnp.maximum(i * (tm // HALO) - 1, 0), 0)),
                  pl.BlockSpec((HALO, D_MODEL), lambda i: (jnp.minimum((i + 1) * (tm // HALO), nbh - 1), 0)),
                  pe_spec(lambda i: jnp.minimum(i, nb0 - 1)), pe_spec(lambda i: jnp.maximum(i - nb0, 0)),
                  full(wup), full(cw), full(cb), full(wdn), full(wpg), full(wpp), full(g2), full(b2)],
        out_specs=row(D_MODEL),
        out_shape=jax.ShapeDtypeStruct((n, D_MODEL), F32),
        scratch_shapes=[pltpu.VMEM((2, tm + 2 * HALO, 2 * FF_BLK), F32), pltpu.VMEM((tm, D_FF), BF16)],
        compiler_params=_cparams(("parallel",)),
        name=f"ffn_l{layer}",
    )(x, x, x, pe[0], pe[1], wup, cw, cb, wdn, wpg, wpp, g2, b2)


def kernel(x_prompt, x_sample, p_prompt, p_sample, w_in, q_norm_g, kv_norm_g, w_uq, w_ukv, tshift_mu, w0, w_lora_up, a0, a_lora_up, g_lora_up, k_k, k_a, r_k, v0, v_lora_down, v_lora_up, lnx_g, lnx_b, w_pa, w_pb, w_o, ln1_g, ln1_b, w_ffn_up, conv_w, conv_b, w_ffn_down, w_pe_gate, w_pe_proj, ln2_g, ln2_b):
    depth = w_in.shape[0]
    alpha = (2 * depth) ** 0.25
    b0, t0, _ = x_prompt.shape
    b1, t1, _ = x_sample.shape
    grp = _Groups(b0, t0, b1, t1)
    n = grp.n
    params = dict(tshift_mu=tshift_mu, w0=w0, w_lora_up=w_lora_up, a0=a0, a_lora_up=a_lora_up,
                  g_lora_up=g_lora_up, k_k=k_k, k_a=k_a, r_k=r_k, v0=v0, v_lora_down=v_lora_down,
                  v_lora_up=v_lora_up, lnx_g=lnx_g, lnx_b=lnx_b, w_pa=w_pa, w_pb=w_pb, w_o=w_o,
                  ln1_g=ln1_g, ln1_b=ln1_b, w_ffn_up=w_ffn_up, conv_w=conv_w, conv_b=conv_b,
                  w_ffn_down=w_ffn_down, w_pe_gate=w_pe_gate, w_pe_proj=w_pe_proj, ln2_g=ln2_g,
                  ln2_b=ln2_b)

    tmin = min(t0, t1)
    tm = min(256, tmin)
    tq = min(1024, tmin)
    tk = min(2048, tmin)
    x = jnp.concatenate([x_prompt.reshape(b0 * t0, D_MODEL), x_sample.reshape(b1 * t1, D_MODEL)], 0)
    pe = (p_prompt.reshape(depth, b0 * t0, P_DIM), p_sample.reshape(depth, b1 * t1, P_DIM))
    tab = _rope_tables(max(t0, t1))

    v_first = None
    for layer in range(depth):
        q, k, v, r, vv, kk, lw2, kd2, kka2, g, bonus = _front(
            x, tab, v_first, w_in[layer], q_norm_g[layer], kv_norm_g[layer], w_uq[layer], w_ukv[layer],
            params, layer, grp, tm)
        att = [_attention(q, k, v, grp, gi, tq, tk) for gi in range(2)]
        if layer == 0:
            v_first = vv
        y2 = [_rwkv_scan(r, vv, kk, lw2, kd2, kka2, grp, gi, math.gcd(grp.b[gi], SCAN_SEQS)) for gi in range(2)]
        x = _mix(x, att, y2, g, bonus, params, w_in[layer], layer, grp, tm, alpha)
        x = _ffn(x, pe, params, layer, grp, tm, alpha)

    return (x[:grp.n0].reshape(b0, t0, D_MODEL), x[grp.n0:].reshape(b1, t1, D_MODEL))
```

```python
import functools
import math

import jax
import jax.numpy as jnp
from jax import lax
from jax.experimental import pallas as pl
from jax.experimental.pallas import tpu as pltpu

F32 = jnp.float32
BF16 = jnp.bfloat16

D_MODEL = 1024
HEADS = 8
NOPE = 64
ROPE = 32
HALF = ROPE // 2
VDIM = 64
QK = NOPE + ROPE
HPAD = 128
Q_LORA = 768
KV_LORA = 256
ROPE_THETA = 10000.0
RH = 64
RC = HEADS * RH
LORA = 64
GATE_LORA = 128
MV_LORA = 32
DECAY_SCALE = 0.606531
GN_EPS = 64e-5
D_FF = 2816
FF_BLK = 256
N_FF = D_FF // FF_BLK
P_DIM = 256
LN_EPS = 1e-5
RMS_EPS = 1e-6
CHUNK = 64
GRP = 2
BD = GRP * CHUNK
SCAN_SEQS = 8
HALO = 16
ATT_SEGS = 4
ATT_ONES = 16

OFF_CKV = Q_LORA
OFF_KR = OFF_CKV + KV_LORA
OFF_RWKV = OFF_KR + ROPE
RWKV_COLS = 3 * RC + 2 * LORA + 2 * LORA + GATE_LORA
OFF_GA = OFF_RWKV + RWKV_COLS
OFF_GB = OFF_GA + D_MODEL

VMEM_LIMIT = 56 * 1024 * 1024


def _cparams(sem):
    return pltpu.CompilerParams(dimension_semantics=sem, vmem_limit_bytes=VMEM_LIMIT)


def _dot(a, b):
    return jnp.dot(a, b, preferred_element_type=F32)


def _dot_nt(a, b):
    return lax.dot_general(a, b, (((1,), (1,)), ((), ())), preferred_element_type=F32)


def _dot_tn(a, b):
    return lax.dot_general(a, b, (((0,), (0,)), ((), ())), preferred_element_type=F32)


def _split2(x):
    hi = x.astype(BF16)
    lo = (x - hi.astype(F32)).astype(BF16)
    return hi, lo


def _segsum(x, ones_bd):
    hi, lo = _split2(x)
    return _dot(hi, ones_bd) + _dot(lo, ones_bd)


def _sigmoid(x):
    return 1.0 / (1.0 + jnp.exp(-x))


def _layernorm(x, g, b):
    mu = jnp.mean(x, axis=-1, keepdims=True)
    d = x - mu
    var = jnp.mean(d * d, axis=-1, keepdims=True)
    return d * lax.rsqrt(var + LN_EPS) * g + b


class _Groups:
    def __init__(self, b0, t0, b1, t1):
        self.b = (b0, b1)
        self.t = (t0, t1)
        self.n0 = b0 * t0
        self.n = b0 * t0 + b1 * t1
        self.off = (0, self.n0)

    def seq_pos(self, row0):
        in0 = row0 < self.n0
        pos = jnp.where(in0, lax.rem(row0, self.t[0]), lax.rem(row0 - self.n0, self.t[1]))
        tlen = jnp.where(in0, self.t[0], self.t[1])
        return pos, tlen


def _mla_part(xb, tab_ref, w1_ref, qg_ref, kvg_ref, wuq_ref, wuk_ref, wuv_ref, e_ref, q_ref, k_ref, v_ref):
    h1 = _dot(xb, w1_ref[...])
    cq = h1[:, :Q_LORA]
    cqn = cq * lax.rsqrt(jnp.mean(cq * cq, axis=-1, keepdims=True) + RMS_EPS) * qg_ref[...]
    q = _dot(cqn.astype(BF16), wuq_ref[...])
    cq_tab = tab_ref[:, 0:HPAD]
    sa_tab = tab_ref[:, HPAD:2 * HPAD]
    sb_tab = tab_ref[:, 2 * HPAD:3 * HPAD]
    for h in range(HEADS):
        qh = q[:, h * HPAD:(h + 1) * HPAD]
        out = qh * cq_tab + pltpu.roll(qh, HALF, 1) * sa_tab + pltpu.roll(qh, HPAD - HALF, 1) * sb_tab
        q_ref[:, h * HPAD:(h + 1) * HPAD] = out.astype(BF16)

    ckv = h1[:, OFF_CKV:OFF_KR]
    ckvn = ckv * lax.rsqrt(jnp.mean(ckv * ckv, axis=-1, keepdims=True) + RMS_EPS) * kvg_ref[...]
    ckvb = ckvn.astype(BF16)
    kr = h1[:, OFF_KR:OFF_KR + HPAD]
    krs = h1[:, OFF_KR + HPAD:OFF_KR + 2 * HPAD]
    krot = kr * tab_ref[:, 3 * HPAD:4 * HPAD] + krs * tab_ref[:, 4 * HPAD:5 * HPAD]
    k = _dot(ckvb, wuk_ref[...]) + _dot(krot.astype(BF16), e_ref[...])
    k_ref[...] = k.astype(BF16)
    v_ref[...] = _dot_nt(wuv_ref[...], ckvb).astype(BF16)


def _rope_tables(t_max):
    pos = jnp.arange(t_max, dtype=F32)
    inv_freq = ROPE_THETA ** (-jnp.arange(0, ROPE, 2, dtype=F32) / ROPE)
    ang = pos[:, None] * inv_freq[None, :]
    cos, sin = jnp.cos(ang), jnp.sin(ang)
    scale = QK ** -0.5 * math.log2(math.e)
    zeros = lambda n: jnp.zeros((t_max, n), F32)
    cq = jnp.concatenate([jnp.ones((t_max, NOPE), F32), cos, cos, zeros(HPAD - QK)], 1) * scale
    sa = jnp.concatenate([zeros(NOPE + HALF), sin, zeros(HPAD - QK)], 1) * scale
    sb = jnp.concatenate([zeros(NOPE), -sin, zeros(HALF + HPAD - QK)], 1) * scale
    ck = jnp.concatenate([cos, cos, zeros(HPAD - ROPE)], 1)
    sk = jnp.concatenate([-sin, sin, zeros(HPAD - ROPE)], 1)
    return jnp.concatenate([cq, sa, sb, ck, sk], 1)


def _mla_weights(w_in_l, q_norm_g, kv_norm_g, w_uq, w_ukv):
    w_kr = w_in_l[:, OFF_KR:OFF_RWKV]
    pad = jnp.zeros((D_MODEL, HPAD - ROPE), F32)
    w1 = jnp.concatenate([w_in_l[:, :OFF_KR], w_kr, pad,
                          w_kr[:, HALF:], w_kr[:, :HALF], pad], 1).astype(BF16)
    wz = w_in_l[:, OFF_RWKV:OFF_GA].astype(BF16)
    wuq = jnp.pad(w_uq.reshape(Q_LORA, HEADS, QK), ((0, 0), (0, 0), (0, HPAD - QK)))
    wuq = wuq.reshape(Q_LORA, HEADS * HPAD).astype(BF16)
    wkv = w_ukv.reshape(KV_LORA, HEADS, NOPE + VDIM)
    wuk = jnp.pad(wkv[:, :, :NOPE], ((0, 0), (0, 0), (0, HPAD - NOPE)))
    wuk = wuk.reshape(KV_LORA, HEADS * HPAD).astype(BF16)
    wuv = wkv[:, :, NOPE:].reshape(KV_LORA, HEADS * VDIM).T.astype(BF16)
    place = (jnp.arange(HEADS * HPAD)[None, :] % HPAD) == (NOPE + jnp.arange(HPAD)[:, None])
    place = jnp.where(jnp.arange(HPAD)[:, None] < ROPE, place, False).astype(BF16)
    qg = q_norm_g.reshape(1, Q_LORA)
    kvg = kv_norm_g.reshape(1, KV_LORA)
    return wz, [w1, qg, kvg, wuq, wuk, wuv, place]


def _attn_kernel(q_ref, k_ref, vt_ref, o_ref, m_ref, acc_ref):
    ki = pl.program_id(2)

    @pl.when(ki == 0)
    def _():
        m_ref[...] = jnp.full(m_ref.shape, -jnp.inf, F32)
        acc_ref[...] = jnp.zeros(acc_ref.shape, F32)

    tk, tq = k_ref.shape[0], q_ref.shape[0]
    seg = ATT_SEGS if tk % (8 * ATT_SEGS) == 0 else 1

    def colreduce(fn, x):
        part = fn(x.reshape(seg, tk // (8 * seg), 8, tq), axis=1)
        return fn(fn(part, axis=0), axis=0, keepdims=True)

    def scores(h):
        hs = slice(h * HPAD, (h + 1) * HPAD)
        return _dot_nt(k_ref[:, hs], q_ref[:, hs])

    ones_rows = (lax.broadcasted_iota(jnp.int32, (ATT_ONES, tk), 0) == 0).astype(BF16)
    st_next = scores(0)
    for h in range(HEADS):
        st = st_next
        if h + 1 < HEADS:
            st_next = scores(h + 1)
        m_prev = m_ref[h]
        m_new = jnp.maximum(m_prev, colreduce(jnp.max, st))
        alpha = jnp.exp2(m_prev - m_new)
        p = jnp.exp2(st - m_new).astype(BF16)
        m_ref[h] = m_new
        vt1 = jnp.concatenate([vt_ref[h * VDIM:(h + 1) * VDIM, :], ones_rows], axis=0)
        acc_ref[h] = alpha * acc_ref[h] + _dot(vt1, p)

    @pl.when(ki == pl.num_programs(2) - 1)
    def _():
        for h in range(HEADS):
            acc = acc_ref[h]
            o = acc[:VDIM] / acc[VDIM:VDIM + 1]
            o_ref[:, h * VDIM:(h + 1) * VDIM] = o.T.astype(BF16)


def _attention(q, k, vt, grp, gi, tq, tk):
    b, t = grp.b[gi], grp.t[gi]
    nq, nk = t // tq, t // tk
    qoff, koff = grp.off[gi] // tq, grp.off[gi] // tk
    qmap = lambda bi, qi, ki: (qoff + bi * nq + qi, 0)
    kmap = lambda bi, qi, ki: (koff + bi * nk + ki, 0)
    return pl.pallas_call(
        _attn_kernel,
        grid=(b, nq, nk),
        in_specs=[pl.BlockSpec((tq, HEADS * HPAD), qmap), pl.BlockSpec((tk, HEADS * HPAD), kmap),
                  pl.BlockSpec((HEADS * VDIM, tk), lambda bi, qi, ki: (0, koff + bi * nk + ki))],
        out_specs=pl.BlockSpec((tq, HEADS * VDIM), lambda bi, qi, ki: (bi * nq + qi, 0)),
        out_shape=jax.ShapeDtypeStruct((b * t, HEADS * VDIM), BF16),
        scratch_shapes=[pltpu.VMEM((HEADS, 1, tq), F32), pltpu.VMEM((HEADS, VDIM + ATT_ONES, tq), F32)],
        compiler_params=_cparams(("parallel", "parallel", "arbitrary")),
        name=f"attention_g{gi}",
    )(q, k, vt)


def _front_kernel(grp, tm, has_vmix, *refs):
    (x_ref, xp_ref, xn_ref, tab_ref, wz_ref, w1_ref, qg_ref, kvg_ref, wuq_ref, wuk_ref, wuv_ref, e_ref,
     mu_ref, wup_ref, w0_ref, aup_ref, a0_ref, gup_ref, kk_ref, ka_ref, rk_ref, ones_ref) = refs[:22]
    if has_vmix:
        vf_ref, v0_ref, vd_ref, vu_ref = refs[22:26]
    (q_ref, k_ref, v_ref,
     r_out, v_out, kk_out, lw_out, kd_out, kka_out, g_out, bonus_out) = refs[-11:]
    xb = x_ref[...].astype(BF16)
    zt = _dot(xb, wz_ref[...])
    halo = jnp.concatenate([xp_ref[...], xn_ref[...]], axis=0).astype(BF16)
    zh = _dot(halo, wz_ref[...])
    _mla_part(xb, tab_ref, w1_ref, qg_ref, kvg_ref, wuq_ref, wuk_ref, wuv_ref, e_ref, q_ref, k_ref, v_ref)

    pos, tlen = grp.seq_pos(pl.program_id(0) * tm)
    first = pos == 0
    last = pos + tm == tlen
    prev_row = jnp.where(first, 0.0, zh[7:8, :])
    next_row = jnp.where(last, 0.0, zh[8:9, :])
    rows = lax.broadcasted_iota(jnp.int32, zt.shape, 0)
    z_prev = jnp.where(rows == 0, prev_row, pltpu.roll(zt, 1, 0))
    z_next = jnp.where(rows == tm - 1, next_row, pltpu.roll(zt, tm - 1, 0))
    zm = zt + mu_ref[...] * (0.5 * (z_prev + z_next) - zt)

    r = zm[:, 0:RC]
    k = zm[:, RC:2 * RC]
    v = zm[:, 2 * RC:3 * RC]
    o = 3 * RC
    zw = zm[:, o:o + 2 * LORA]
    za = zm[:, o + 2 * LORA:o + 4 * LORA]
    zg = zm[:, o + 4 * LORA:o + 4 * LORA + GATE_LORA]

    lw2 = -DECAY_SCALE * _sigmoid(w0_ref[...] + _dot(jnp.tanh(zw).astype(BF16), wup_ref[...]))
    a2 = _sigmoid(a0_ref[...] + _dot(za.astype(BF16), aup_ref[...]))
    g = _dot(_sigmoid(zg).astype(BF16), gup_ref[...])
    if has_vmix:
        xd = _dot(xb, vd_ref[...])
        gate = _sigmoid(v0_ref[...] + _dot(xd.astype(BF16), vu_ref[...]))
        v = v + (vf_ref[...] - v) * gate

    ones_bd = ones_ref[...]
    kkr = k * kk_ref[...]
    kk = kkr * lax.rsqrt(_segsum(kkr * kkr, ones_bd) + 1e-12)
    ksum = jnp.zeros_like(k)
    for d in range(2):
        a_d = a2[:, d * RC:(d + 1) * RC]
        kd = k * (1.0 + (a_d - 1.0) * ka_ref[...])
        ksum = ksum + kd
        lw_out[d] = lw2[:, d * RC:(d + 1) * RC]
        kd_out[d] = kd
        kka_out[d] = kk * a_d
    r_out[...] = r
    v_out[...] = v
    kk_out[...] = kk
    g_out[...] = g
    bonus_out[...] = _segsum(r * ksum * rk_ref[...], ones_bd) * v


def _blockdiag2(w):
    z = jnp.zeros_like(w[0])
    return jnp.concatenate([jnp.concatenate([w[0], z], 1), jnp.concatenate([z, w[1]], 1)], 0)


def _ones_bd():
    idx = jnp.arange(RC) // RH
    return (idx[:, None] == idx[None, :]).astype(BF16)


def _front(x, tab, v_first, w_in_l, q_norm_g, kv_norm_g, w_uq, w_ukv, p, layer, grp, tm):
    n = grp.n
    has_vmix = layer > 0
    wz, mla_w = _mla_weights(w_in_l, q_norm_g, kv_norm_g, w_uq, w_ukv)
    row2 = lambda a: a.reshape(1, -1)
    mu = row2(p['tshift_mu'][layer])
    wup = _blockdiag2(p['w_lora_up'][layer]).astype(BF16)
    w0 = row2(p['w0'][layer])
    aup = _blockdiag2(p['a_lora_up'][layer]).astype(BF16)
    a0 = row2(p['a0'][layer])
    gup = p['g_lora_up'][layer].astype(BF16)
    kk_p, ka_p, rk_p = row2(p['k_k'][layer]), row2(p['k_a'][layer]), row2(p['r_k'][layer])
    ones_bd = _ones_bd()

    nb8 = n // 8
    bt0, bt1 = grp.t[0] // tm, grp.t[1] // tm
    nb0 = grp.n0 // tm

    def tab_map(i):
        return (jnp.where(i < nb0, lax.rem(i, bt0), lax.rem(i - nb0, bt1)), 0)

    full = lambda a: pl.BlockSpec(a.shape, lambda i: (0,) * a.ndim, pipeline_mode=pl.Buffered(1))
    row = lambda c: pl.BlockSpec((tm, c), lambda i: (i, 0))
    row_d = pl.BlockSpec((2, tm, RC), lambda i: (0, i, 0))
    weights = [wz] + mla_w + [mu, wup, w0, aup, a0, gup, kk_p, ka_p, rk_p, ones_bd]
    in_specs = [row(D_MODEL),
                pl.BlockSpec((8, D_MODEL), lambda i: (jnp.maximum(i * (tm // 8) - 1, 0), 0)),
                pl.BlockSpec((8, D_MODEL), lambda i: (jnp.minimum((i + 1) * (tm // 8), nb8 - 1), 0)),
                pl.BlockSpec((tm, 5 * HPAD), tab_map)] + [full(w) for w in weights]
    args = [x, x, x, tab] + weights
    if has_vmix:
        v0 = row2(p['v0'][layer - 1])
        vd = jnp.pad(p['v_lora_down'][layer - 1], ((0, 0), (0, HPAD - MV_LORA))).astype(BF16)
        vu = jnp.pad(p['v_lora_up'][layer - 1], ((0, HPAD - MV_LORA), (0, 0))).astype(BF16)
        in_specs += [row(RC), full(v0), full(vd), full(vu)]
        args += [v_first, v0, vd, vu]
    qk = jax.ShapeDtypeStruct((n, HEADS * HPAD), BF16)
    one = jax.ShapeDtypeStruct((n, RC), F32)
    two = jax.ShapeDtypeStruct((2, n, RC), F32)
    return pl.pallas_call(
        functools.partial(_front_kernel, grp, tm, has_vmix),
        grid=(n // tm,),
        in_specs=in_specs,
        out_specs=[row(HEADS * HPAD), row(HEADS * HPAD), pl.BlockSpec((HEADS * VDIM, tm), lambda i: (0, i)),
                   row(RC), row(RC), row(RC), row_d, row_d, row_d, row(RC), row(RC)],
        out_shape=[qk, qk, jax.ShapeDtypeStruct((HEADS * VDIM, n), BF16),
                   one, one, one, two, two, two, one, one],
        compiler_params=_cparams(("parallel",)),
        name=f"front_l{layer}",
    )(*args)


def _scan_kernel(nseq, *refs):
    ins = [refs[6 * i:6 * i + 6] for i in range(nseq)]
    y_ref, s_ref = refs[6 * nseq:]
    d = pl.program_id(1)
    c = pl.program_id(2)

    @pl.when(c == 0)
    def _():
        s_ref[...] = jnp.zeros(s_ref.shape, F32)

    sgn = jnp.where(d == 0, 1, -1)
    t_i = lax.broadcasted_iota(jnp.int32, (CHUNK, CHUNK), 0)
    t_j = lax.broadcasted_iota(jnp.int32, (CHUNK, CHUNK), 1)
    cum_mask = jnp.where((t_i - t_j) * sgn >= 0, 1.0, 0.0).astype(BF16)

    bi = lax.broadcasted_iota(jnp.int32, (BD, BD), 0)
    bj = lax.broadcasted_iota(jnp.int32, (BD, BD), 1)
    shift = int(math.log2(CHUNK))
    same_head = (bi >> shift) == (bj >> shift)
    lag = ((bi & (CHUNK - 1)) - (bj & (CHUNK - 1))) * sgn
    strict = same_head & (lag > 0)
    incl = same_head & (lag >= 0)
    eye = jnp.where(bi == bj, 1.0, 0.0)

    def bd(x):
        return jnp.where(same_head, jnp.concatenate([x] * GRP, axis=0), 0.0).astype(BF16)

    pre = []
    for r_ref, v_ref, kk_ref, lw_ref, kd_ref, kka_ref in ins:
        lw = lw_ref[...]
        hi = lw.astype(BF16)
        r1 = lw - hi.astype(F32)
        mid = r1.astype(BF16)
        lo = (r1 - mid.astype(F32)).astype(BF16)
        cs = _dot(cum_mask, hi) + _dot(cum_mask, mid) + _dot(cum_mask, lo)
        gin = jnp.exp(-cs)
        pre.append(dict(a=jnp.exp(cs - lw) * kk_ref[...], b=-kka_ref[...] * gin, k=kd_ref[...] * gin,
                        q=jnp.exp(cs) * r_ref[...], v=v_ref[...],
                        g_last=jnp.exp(jnp.sum(lw, axis=0, keepdims=True))))

    chains = [(i, g) for i in range(nseq) for g in range(HEADS // GRP)]
    st = []
    for i, g in chains:
        cols = slice(g * BD, (g + 1) * BD)
        e = {nm: bd(pre[i][nm][:, cols]) for nm in ('a', 'b', 'k', 'q', 'v')}
        e['g_last'] = pre[i]['g_last'][:, cols]
        st.append(e)
    for e in st:
        prod = _dot_nt(jnp.concatenate([e['a'], e['q']], axis=0), jnp.concatenate([e['b'], e['k']], axis=0))
        e['pw'] = jnp.where(strict, prod[:BD, :BD], 0.0)
        e['tinv'] = eye + e['pw']
        e['l_ak'] = jnp.where(strict, prod[:BD, BD:], 0.0).astype(BF16)
        e['l_qb'] = jnp.where(incl, prod[BD:, :BD], 0.0).astype(BF16)
        e['l_qk'] = jnp.where(incl, prod[BD:, BD:], 0.0).astype(BF16)

    for e in st:
        pwb = e['pw'].astype(BF16)
        e['pw'] = _dot(pwb, pwb)
    for _ in range(int(math.log2(CHUNK)) - 2):
        for e in st:
            pwb = e['pw'].astype(BF16)
            both = _dot(pwb, jnp.concatenate([pwb, e['tinv'].astype(BF16)], axis=1))
            e['pw'] = both[:, :BD]
            e['tinv'] = e['tinv'] + both[:, BD:]
    for e in st:
        e['tinv'] = e['tinv'] + _dot(e['pw'].astype(BF16), e['tinv'].astype(BF16))

    for n, e in enumerate(st):
        e['s0'] = s_ref[n]
        e['s0t'] = e['s0'].T.astype(BF16)
        e['rhs'] = _dot(jnp.concatenate([e['a'], e['l_ak']], axis=1),
                        jnp.concatenate([e['s0t'], e['v']], axis=0))
    for e in st:
        e['ub'] = _dot(e['tinv'].astype(BF16), e['rhs'].astype(BF16)).astype(BF16)
    for (i, g), e in zip(chains, st):
        y = _dot(jnp.concatenate([e['q'], e['l_qb'], e['l_qk']], axis=1),
                 jnp.concatenate([e['s0t'], e['ub'], e['v']], axis=0))
        y_ref[i, :, g * BD:(g + 1) * BD] = sum(y[h * CHUNK:(h + 1) * CHUNK] for h in range(GRP))
    for n, e in enumerate(st):
        upd = _dot_tn(jnp.concatenate([e['ub'], e['v']], axis=0), jnp.concatenate([e['b'], e['k']], axis=0))
        s_ref[n] = (e['s0'] + upd) * e['g_last']


def _rwkv_scan(r, v, kk, lw2, kd2, kka2, grp, gi, nseq):
    b, t = grp.b[gi], grp.t[gi]
    assert b % nseq == 0
    nc = t // CHUNK
    off = grp.off[gi] // CHUNK

    def blk(bi, d, c, i):
        return (bi * nseq + i) * nc + jnp.where(d == 0, c, nc - 1 - c)

    in_specs, args = [], []
    for i in range(nseq):
        one = pl.BlockSpec((CHUNK, RC), lambda bi, d, c, i=i: (off + blk(bi, d, c, i), 0))
        two = pl.BlockSpec((None, CHUNK, RC), lambda bi, d, c, i=i: (d, off + blk(bi, d, c, i), 0))
        in_specs += [one, one, one, two, two, two]
        args += [r, v, kk, lw2, kd2, kka2]
    y = pl.pallas_call(
        functools.partial(_scan_kernel, nseq),
        grid=(b // nseq, 2, nc),
        in_specs=in_specs,
        out_specs=pl.BlockSpec((None, None, nseq, CHUNK, RC),
                               lambda bi, d, c: (d, bi, 0, jnp.where(d == 0, c, nc - 1 - c), 0)),
        out_shape=jax.ShapeDtypeStruct((2, b // nseq, nseq, t, RC), F32),
        scratch_shapes=[pltpu.VMEM((nseq * (HEADS // GRP), BD, BD), F32)],
        compiler_params=_cparams(("parallel", "arbitrary", "arbitrary")),
        name=f"rwkv_scan_g{gi}",
    )(*args)
    return y.reshape(2, b * t, RC)


def _mix_kernel(alpha, nb0, x_ref, att0_ref, att1_ref, y0_ref, y1_ref, g_ref, bonus_ref, lng_ref, lnb_ref,
                ones_ref, wg_ref, wpa_ref, wpb_ref, wo_ref, g1_ref, b1_ref, o_ref):
    in0 = pl.program_id(0) < nb0
    att = jnp.where(in0, att0_ref[...], att1_ref[...])
    ones_bd = ones_ref[...]
    y = jnp.where(in0, y0_ref[0] + y0_ref[1], y1_ref[0] + y1_ref[1])
    mean = _segsum(y, ones_bd) * (1.0 / RH)
    dlt = y - mean
    var = _segsum(dlt * dlt, ones_bd) * (1.0 / RH)
    yn = dlt * lax.rsqrt(var + GN_EPS) * lng_ref[...] + lnb_ref[...]
    rw = ((yn + bonus_ref[...]) * g_ref[...]).astype(BF16)

    x = x_ref[...]
    gates = _dot(x.astype(BF16), wg_ref[...])
    mixed = (_sigmoid(gates[:, :D_MODEL]) * _dot(att, wpa_ref[...])
             + _sigmoid(gates[:, D_MODEL:]) * _dot(rw, wpb_ref[...]))
    o_ref[...] = _layernorm(alpha * x + _dot(mixed.astype(BF16), wo_ref[...]), g1_ref[...], b1_ref[...])


def _mix(x, att, y2, g, bonus, p, w_in_l, layer, grp, tm, alpha):
    n = grp.n
    nb0, nb1 = grp.n0 // tm, (grp.n - grp.n0) // tm
    idx0 = lambda i: jnp.minimum(i, nb0 - 1)
    idx1 = lambda i: jnp.maximum(i - nb0, 0)
    row2 = lambda a: a.reshape(1, -1)
    lng, lnb = row2(p['lnx_g'][layer]), row2(p['lnx_b'][layer])
    ones_bd = _ones_bd()
    wg = w_in_l[:, OFF_GA:].astype(BF16)
    wpa, wpb, wo = (p[nm][layer].astype(BF16) for nm in ('w_pa', 'w_pb', 'w_o'))
    g1, b1 = row2(p['ln1_g'][layer]), row2(p['ln1_b'][layer])
    full = lambda a: pl.BlockSpec(a.shape, lambda i: (0,) * a.ndim, pipeline_mode=pl.Buffered(1))
    row = lambda c: pl.BlockSpec((tm, c), lambda i: (i, 0))
    return pl.pallas_call(
        functools.partial(_mix_kernel, alpha, nb0),
        grid=(n // tm,),
        in_specs=[row(D_MODEL),
                  pl.BlockSpec((tm, RC), lambda i: (idx0(i), 0)), pl.BlockSpec((tm, RC), lambda i: (idx1(i), 0)),
                  pl.BlockSpec((2, tm, RC), lambda i: (0, idx0(i), 0)),
                  pl.BlockSpec((2, tm, RC), lambda i: (0, idx1(i), 0)),
                  row(RC), row(RC),
                  full(lng), full(lnb), full(ones_bd), full(wg), full(wpa), full(wpb), full(wo),
                  full(g1), full(b1)],
        out_specs=row(D_MODEL),
        out_shape=jax.ShapeDtypeStruct((n, D_MODEL), F32),
        compiler_params=_cparams(("parallel",)),
        name=f"mix_l{layer}",
    )(x, att[0], att[1], y2[0], y2[1], g, bonus, lng, lnb, ones_bd, wg, wpa, wpb, wo, g1, b1)


def _ffn_kernel(grp, tm, alpha, tile0, x_ref, xp_ref, xn_ref, p0_ref, p1_ref, wup_ref, cw_ref, cb_ref, wdn_ref,
                wpg_ref, wpp_ref, g2_ref, b2_ref, o_ref, ue_ref, hg_ref):
    row0 = (pl.program_id(0) + tile0) * tm
    pos, tlen = grp.seq_pos(row0)
    pemb = jnp.where(row0 < grp.n0, p0_ref[...], p1_ref[...])
    keep_prev = (pos != 0).astype(F32)
    keep_next = (pos + tm != tlen).astype(F32)
    x = x_ref[...]
    xb = x.astype(BF16)
    xe = jnp.concatenate([(xp_ref[...] * keep_prev).astype(BF16), xb,
                          (xn_ref[...] * keep_next).astype(BF16)], axis=0)
    acc = alpha * x + _sigmoid(_dot(xb, wpg_ref[...])) * _dot(pemb.astype(BF16), wpp_ref[...])
    ue_ref[0] = _dot(xe, wup_ref[0])
    for j in range(N_FF):
        buf = j % 2
        if j + 1 < N_FF:
            ue_ref[1 - buf] = _dot(xe, wup_ref[j + 1])
        cw = cw_ref[j]
        ue = ue_ref[buf]
        rows = ue.shape[0]
        u_prev = pltpu.roll(ue, 1, 0)[HALO:HALO + tm]
        u_next = pltpu.roll(ue, rows - 1, 0)[HALO:HALO + tm]
        u = cw[0:1] * u_prev + cw[1:2] * ue[HALO:HALO + tm] + cw[2:3] * u_next + cb_ref[j]
        hg = jax.nn.gelu(u[:, :FF_BLK]) * u[:, FF_BLK:]
        hg_ref[:, j * FF_BLK:(j + 1) * FF_BLK] = hg.astype(BF16)
    acc = acc + _dot(hg_ref[...], wdn_ref[...])
    o_ref[...] = _layernorm(acc, g2_ref[...], b2_ref[...])


def _ffn(x, pe, p, layer, grp, tm, alpha, gi=None):
    n = grp.n
    tile0 = 0 if gi is None else grp.off[gi] // tm
    ntiles = n // tm if gi is None else grp.b[gi] * grp.t[gi] // tm
    row2 = lambda a: a.reshape(1, -1)

    def interleave(a):
        lead = a.shape[:-1]
        a = a.reshape(lead + (2, N_FF, FF_BLK))
        a = jnp.moveaxis(a, -2, 0)
        return a.reshape((N_FF,) + lead + (2 * FF_BLK,))

    wup = interleave(p['w_ffn_up'][layer]).astype(BF16)
    cw = jnp.pad(interleave(p['conv_w'][layer]), ((0, 0), (0, 5), (0, 0)))
    cb = interleave(p['conv_b'][layer].reshape(1, -1))
    wdn = p['w_ffn_down'][layer].astype(BF16)
    wpg = p['w_pe_gate'][layer].astype(BF16)
    wpp = p['w_pe_proj'][layer].astype(BF16)
    g2, b2 = row2(p['ln2_g'][layer]), row2(p['ln2_b'][layer])

    nbh = n // HALO
    nb0 = grp.n0 // tm
    pe_spec = lambda idx: pl.BlockSpec((None, tm, P_DIM), lambda i: (layer, idx(i + tile0), 0))
    full = lambda a: pl.BlockSpec(a.shape, lambda i: (0,) * a.ndim, pipeline_mode=pl.Buffered(1))
    hpt = tm // HALO
    return pl.pallas_call(
        functools.partial(_ffn_kernel, grp, tm, alpha, tile0),
        grid=(ntiles,),
        in_specs=[pl.BlockSpec((tm, D_MODEL), lambda i: (i + tile0, 0)),
                  pl.BlockSpec((HALO, D_MODEL), lambda i: (jnp.maximum((i + tile0) * hpt - 1, 0), 0)),
                  pl.BlockSpec((HALO, D_MODEL), lambda i: (jnp.minimum((i + tile0 + 1) * hpt, nbh - 1), 0)),
                  pe_spec(lambda i: jnp.minimum(i, nb0 - 1)), pe_spec(lambda i: jnp.maximum(i - nb0, 0)),
                  full(wup), full(cw), full(cb), full(wdn), full(wpg), full(wpp), full(g2), full(b2)],
        out_specs=pl.BlockSpec((tm, D_MODEL), lambda i: (i, 0)),
        out_shape=jax.ShapeDtypeStruct((ntiles * tm, D_MODEL), F32),
        scratch_shapes=[pltpu.VMEM((2, tm + 2 * HALO, 2 * FF_BLK), F32), pltpu.VMEM((tm, D_FF), BF16)],
        compiler_params=_cparams(("parallel",)),
        name=f"ffn_l{layer}" + ("" if gi is None else f"_g{gi}"),
    )(x, x, x, pe[0], pe[1], wup, cw, cb, wdn, wpg, wpp, g2, b2)


def kernel(x_prompt, x_sample, p_prompt, p_sample, w_in, q_norm_g, kv_norm_g, w_uq, w_ukv, tshift_mu, w0, w_lora_up, a0, a_lora_up, g_lora_up, k_k, k_a, r_k, v0, v_lora_down, v_lora_up, lnx_g, lnx_b, w_pa, w_pb, w_o, ln1_g, ln1_b, w_ffn_up, conv_w, conv_b, w_ffn_down, w_pe_gate, w_pe_proj, ln2_g, ln2_b):
    depth = w_in.shape[0]
    alpha = (2 * depth) ** 0.25
    b0, t0, _ = x_prompt.shape
    b1, t1, _ = x_sample.shape
    grp = _Groups(b0, t0, b1, t1)
    params = dict(tshift_mu=tshift_mu, w0=w0, w_lora_up=w_lora_up, a0=a0, a_lora_up=a_lora_up,
                  g_lora_up=g_lora_up, k_k=k_k, k_a=k_a, r_k=r_k, v0=v0, v_lora_down=v_lora_down,
                  v_lora_up=v_lora_up, lnx_g=lnx_g, lnx_b=lnx_b, w_pa=w_pa, w_pb=w_pb, w_o=w_o,
                  ln1_g=ln1_g, ln1_b=ln1_b, w_ffn_up=w_ffn_up, conv_w=conv_w, conv_b=conv_b,
                  w_ffn_down=w_ffn_down, w_pe_gate=w_pe_gate, w_pe_proj=w_pe_proj, ln2_g=ln2_g,
                  ln2_b=ln2_b)

    tmin = min(t0, t1)
    tm = min(256, tmin)
    tm2 = min(512, tmin)
    tq = min(1024, tmin)
    tk = min(2048, tmin)
    x = jnp.concatenate([x_prompt.reshape(b0 * t0, D_MODEL), x_sample.reshape(b1 * t1, D_MODEL)], 0)
    pe = (p_prompt.reshape(depth, b0 * t0, P_DIM), p_sample.reshape(depth, b1 * t1, P_DIM))
    tab = _rope_tables(max(t0, t1))

    v_first = None
    for layer in range(depth):
        q, k, v, r, vv, kk, lw2, kd2, kka2, g, bonus = _front(
            x, tab, v_first, w_in[layer], q_norm_g[layer], kv_norm_g[layer], w_uq[layer], w_ukv[layer],
            params, layer, grp, tm)
        att = [_attention(q, k, v, grp, gi, tq, tk) for gi in range(2)]
        if layer == 0:
            v_first = vv
        y2 = [_rwkv_scan(r, vv, kk, lw2, kd2, kka2, grp, gi, math.gcd(grp.b[gi], SCAN_SEQS)) for gi in range(2)]
        x = _mix(x, att, y2, g, bonus, params, w_in[layer], layer, grp, tm2, alpha)
        if layer + 1 < depth:
            x = _ffn(x, pe, params, layer, grp, tm2, alpha)

    y0, y1 = (_ffn(x, pe, params, depth - 1, grp, tm2, alpha, gi) for gi in range(2))
    return (y0.reshape(b0, t0, D_MODEL), y1.reshape(b1, t1, D_MODEL))
```

```python
import functools
import math

import jax
import jax.numpy as jnp
from jax import lax
from jax.experimental import pallas as pl
from jax.experimental.pallas import tpu as pltpu

F32 = jnp.float32
BF16 = jnp.bfloat16

D_MODEL = 1024
HEADS = 8
NOPE = 64
ROPE = 32
HALF = ROPE // 2
VDIM = 64
QK = NOPE + ROPE
HPAD = 128
Q_LORA = 768
KV_LORA = 256
ROPE_THETA = 10000.0
RH = 64
RC = HEADS * RH
LORA = 64
GATE_LORA = 128
MV_LORA = 32
DECAY_SCALE = 0.606531
GN_EPS = 64e-5
D_FF = 2816
FF_BLK = 256
N_FF = D_FF // FF_BLK
P_DIM = 256
LN_EPS = 1e-5
RMS_EPS = 1e-6
CHUNK = 64
GRP = 2
BD = GRP * CHUNK
SCAN_SEQS = 8
HALO = 16
ATT_SEGS = 4
ATT_ONES = 16

OFF_CKV = Q_LORA
OFF_KR = OFF_CKV + KV_LORA
OFF_RWKV = OFF_KR + ROPE
RWKV_COLS = 3 * RC + 2 * LORA + 2 * LORA + GATE_LORA
OFF_GA = OFF_RWKV + RWKV_COLS
OFF_GB = OFF_GA + D_MODEL

VMEM_LIMIT = 56 * 1024 * 1024


def _cparams(sem):
    return pltpu.CompilerParams(dimension_semantics=sem, vmem_limit_bytes=VMEM_LIMIT)


def _dot(a, b):
    return jnp.dot(a, b, preferred_element_type=F32)


def _dot_nt(a, b):
    return lax.dot_general(a, b, (((1,), (1,)), ((), ())), preferred_element_type=F32)


def _dot_tn(a, b):
    return lax.dot_general(a, b, (((0,), (0,)), ((), ())), preferred_element_type=F32)


def _split2(x):
    hi = x.astype(BF16)
    lo = (x - hi.astype(F32)).astype(BF16)
    return hi, lo


def _segsum(x, ones_bd, split=False):
    if not split:
        return _dot(x.astype(BF16), ones_bd)
    hi, lo = _split2(x)
    return _dot(hi, ones_bd) + _dot(lo, ones_bd)


def _sigmoid(x):
    return 1.0 / (1.0 + jnp.exp(-x))


def _layernorm(x, g, b):
    mu = jnp.mean(x, axis=-1, keepdims=True)
    d = x - mu
    var = jnp.mean(d * d, axis=-1, keepdims=True)
    return d * lax.rsqrt(var + LN_EPS) * g + b


class _Groups:
    def __init__(self, b0, t0, b1, t1):
        self.b = (b0, b1)
        self.t = (t0, t1)
        self.n0 = b0 * t0
        self.n = b0 * t0 + b1 * t1
        self.off = (0, self.n0)

    def seq_pos(self, row0):
        in0 = row0 < self.n0
        pos = jnp.where(in0, lax.rem(row0, self.t[0]), lax.rem(row0 - self.n0, self.t[1]))
        tlen = jnp.where(in0, self.t[0], self.t[1])
        return pos, tlen


def _mla_part(xb, tab_ref, w1_ref, qg_ref, kvg_ref, wuq_ref, wuk_ref, wuv_ref, e_ref, q_ref, k_ref, v_ref):
    h1 = _dot(xb, w1_ref[...])
    cq = h1[:, :Q_LORA]
    cqn = cq * lax.rsqrt(jnp.mean(cq * cq, axis=-1, keepdims=True) + RMS_EPS) * qg_ref[...]
    q = _dot(cqn.astype(BF16), wuq_ref[...])
    cq_tab = tab_ref[:, 0:HPAD]
    sa_tab = tab_ref[:, HPAD:2 * HPAD]
    sb_tab = tab_ref[:, 2 * HPAD:3 * HPAD]
    for h in range(HEADS):
        qh = q[:, h * HPAD:(h + 1) * HPAD]
        out = qh * cq_tab + pltpu.roll(qh, HALF, 1) * sa_tab + pltpu.roll(qh, HPAD - HALF, 1) * sb_tab
        q_ref[:, h * HPAD:(h + 1) * HPAD] = out.astype(BF16)

    ckv = h1[:, OFF_CKV:OFF_KR]
    ckvn = ckv * lax.rsqrt(jnp.mean(ckv * ckv, axis=-1, keepdims=True) + RMS_EPS) * kvg_ref[...]
    ckvb = ckvn.astype(BF16)
    kr = h1[:, OFF_KR:OFF_KR + HPAD]
    krs = h1[:, OFF_KR + HPAD:OFF_KR + 2 * HPAD]
    krot = kr * tab_ref[:, 3 * HPAD:4 * HPAD] + krs * tab_ref[:, 4 * HPAD:5 * HPAD]
    k = _dot(ckvb, wuk_ref[...]) + _dot(krot.astype(BF16), e_ref[...])
    k_ref[...] = k.astype(BF16)
    v_ref[...] = _dot_nt(wuv_ref[...], ckvb).astype(BF16)


def _rope_tables(t_max):
    pos = jnp.arange(t_max, dtype=F32)
    inv_freq = ROPE_THETA ** (-jnp.arange(0, ROPE, 2, dtype=F32) / ROPE)
    ang = pos[:, None] * inv_freq[None, :]
    cos, sin = jnp.cos(ang), jnp.sin(ang)
    scale = QK ** -0.5 * math.log2(math.e)
    zeros = lambda n: jnp.zeros((t_max, n), F32)
    cq = jnp.concatenate([jnp.ones((t_max, NOPE), F32), cos, cos, zeros(HPAD - QK)], 1) * scale
    sa = jnp.concatenate([zeros(NOPE + HALF), sin, zeros(HPAD - QK)], 1) * scale
    sb = jnp.concatenate([zeros(NOPE), -sin, zeros(HALF + HPAD - QK)], 1) * scale
    ck = jnp.concatenate([cos, cos, zeros(HPAD - ROPE)], 1)
    sk = jnp.concatenate([-sin, sin, zeros(HPAD - ROPE)], 1)
    return jnp.concatenate([cq, sa, sb, ck, sk], 1)


def _mla_weights(w_in_l, q_norm_g, kv_norm_g, w_uq, w_ukv):
    w_kr = w_in_l[:, OFF_KR:OFF_RWKV]
    pad = jnp.zeros((D_MODEL, HPAD - ROPE), F32)
    w1 = jnp.concatenate([w_in_l[:, :OFF_KR], w_kr, pad,
                          w_kr[:, HALF:], w_kr[:, :HALF], pad], 1).astype(BF16)
    wz = w_in_l[:, OFF_RWKV:OFF_GA].astype(BF16)
    wuq = jnp.pad(w_uq.reshape(Q_LORA, HEADS, QK), ((0, 0), (0, 0), (0, HPAD - QK)))
    wuq = wuq.reshape(Q_LORA, HEADS * HPAD).astype(BF16)
    wkv = w_ukv.reshape(KV_LORA, HEADS, NOPE + VDIM)
    wuk = jnp.pad(wkv[:, :, :NOPE], ((0, 0), (0, 0), (0, HPAD - NOPE)))
    wuk = wuk.reshape(KV_LORA, HEADS * HPAD).astype(BF16)
    wuv = wkv[:, :, NOPE:].reshape(KV_LORA, HEADS * VDIM).T.astype(BF16)
    place = (jnp.arange(HEADS * HPAD)[None, :] % HPAD) == (NOPE + jnp.arange(HPAD)[:, None])
    place = jnp.where(jnp.arange(HPAD)[:, None] < ROPE, place, False).astype(BF16)
    qg = q_norm_g.reshape(1, Q_LORA)
    kvg = kv_norm_g.reshape(1, KV_LORA)
    return wz, [w1, qg, kvg, wuq, wuk, wuv, place]


def _attn_kernel(q_ref, k_ref, vt_ref, o_ref, m_ref, acc_ref):
    ki = pl.program_id(2)

    @pl.when(ki == 0)
    def _():
        m_ref[...] = jnp.full(m_ref.shape, -jnp.inf, F32)
        acc_ref[...] = jnp.zeros(acc_ref.shape, F32)

    tk, tq = k_ref.shape[0], q_ref.shape[0]
    seg = ATT_SEGS if tk % (8 * ATT_SEGS) == 0 else 1

    def colreduce(fn, x):
        part = fn(x.reshape(seg, tk // (8 * seg), 8, tq), axis=1)
        return fn(fn(part, axis=0), axis=0, keepdims=True)

    def scores(h):
        hs = slice(h * HPAD, (h + 1) * HPAD)
        return _dot_nt(k_ref[:, hs], q_ref[:, hs])

    ones_rows = (lax.broadcasted_iota(jnp.int32, (ATT_ONES, tk), 0) == 0).astype(BF16)
    st_next = scores(0)
    for h in range(HEADS):
        st = st_next
        if h + 1 < HEADS:
            st_next = scores(h + 1)
        m_prev = m_ref[h]
        m_new = jnp.maximum(m_prev, colreduce(jnp.max, st))
        alpha = jnp.exp2(m_prev - m_new)
        p = jnp.exp2(st - m_new).astype(BF16)
        m_ref[h] = m_new
        vt1 = jnp.concatenate([vt_ref[h * VDIM:(h + 1) * VDIM, :], ones_rows], axis=0)
        acc_ref[h] = alpha * acc_ref[h] + _dot(vt1, p)

    @pl.when(ki == pl.num_programs(2) - 1)
    def _():
        for h in range(HEADS):
            acc = acc_ref[h]
            o = acc[:VDIM] / acc[VDIM:VDIM + 1]
            o_ref[:, h * VDIM:(h + 1) * VDIM] = o.T.astype(BF16)


def _attention(q, k, vt, grp, gi, tq, tk):
    b, t = grp.b[gi], grp.t[gi]
    nq, nk = t // tq, t // tk
    qoff, koff = grp.off[gi] // tq, grp.off[gi] // tk
    qmap = lambda bi, qi, ki: (qoff + bi * nq + qi, 0)
    kmap = lambda bi, qi, ki: (koff + bi * nk + ki, 0)
    return pl.pallas_call(
        _attn_kernel,
        grid=(b, nq, nk),
        in_specs=[pl.BlockSpec((tq, HEADS * HPAD), qmap), pl.BlockSpec((tk, HEADS * HPAD), kmap),
                  pl.BlockSpec((HEADS * VDIM, tk), lambda bi, qi, ki: (0, koff + bi * nk + ki))],
        out_specs=pl.BlockSpec((tq, HEADS * VDIM), lambda bi, qi, ki: (bi * nq + qi, 0)),
        out_shape=jax.ShapeDtypeStruct((b * t, HEADS * VDIM), BF16),
        scratch_shapes=[pltpu.VMEM((HEADS, 1, tq), F32), pltpu.VMEM((HEADS, VDIM + ATT_ONES, tq), F32)],
        compiler_params=_cparams(("parallel", "parallel", "arbitrary")),
        name=f"attention_g{gi}",
    )(q, k, vt)


def _front_kernel(grp, tm, has_vmix, *refs):
    (x_ref, xp_ref, xn_ref, tab_ref, wz_ref, w1_ref, qg_ref, kvg_ref, wuq_ref, wuk_ref, wuv_ref, e_ref,
     mu_ref, wup_ref, w0_ref, aup_ref, a0_ref, gup_ref, kk_ref, ka_ref, rk_ref, ones_ref) = refs[:22]
    if has_vmix:
        vf_ref, v0_ref, vd_ref, vu_ref = refs[22:26]
    (q_ref, k_ref, v_ref,
     r_out, v_out, kk_out, lw_out, kd_out, kka_out, g_out, bonus_out) = refs[-11:]
    xb = x_ref[...].astype(BF16)
    zt = _dot(xb, wz_ref[...])
    halo = jnp.concatenate([xp_ref[...], xn_ref[...]], axis=0).astype(BF16)
    zh = _dot(halo, wz_ref[...])
    _mla_part(xb, tab_ref, w1_ref, qg_ref, kvg_ref, wuq_ref, wuk_ref, wuv_ref, e_ref, q_ref, k_ref, v_ref)

    pos, tlen = grp.seq_pos(pl.program_id(0) * tm)
    first = pos == 0
    last = pos + tm == tlen
    prev_row = jnp.where(first, 0.0, zh[7:8, :])
    next_row = jnp.where(last, 0.0, zh[8:9, :])
    rows = lax.broadcasted_iota(jnp.int32, zt.shape, 0)
    z_prev = jnp.where(rows == 0, prev_row, pltpu.roll(zt, 1, 0))
    z_next = jnp.where(rows == tm - 1, next_row, pltpu.roll(zt, tm - 1, 0))
    zm = zt + mu_ref[...] * (0.5 * (z_prev + z_next) - zt)

    r = zm[:, 0:RC]
    k = zm[:, RC:2 * RC]
    v = zm[:, 2 * RC:3 * RC]
    o = 3 * RC
    zw = zm[:, o:o + 2 * LORA]
    za = zm[:, o + 2 * LORA:o + 4 * LORA]
    zg = zm[:, o + 4 * LORA:o + 4 * LORA + GATE_LORA]

    lw2 = -DECAY_SCALE * _sigmoid(w0_ref[...] + _dot(jnp.tanh(zw).astype(BF16), wup_ref[...]))
    a2 = _sigmoid(a0_ref[...] + _dot(za.astype(BF16), aup_ref[...]))
    g = _dot(_sigmoid(zg).astype(BF16), gup_ref[...])
    if has_vmix:
        xd = _dot(xb, vd_ref[...])
        gate = _sigmoid(v0_ref[...] + _dot(xd.astype(BF16), vu_ref[...]))
        v = v + (vf_ref[...] - v) * gate

    ones_bd = ones_ref[...]
    kkr = k * kk_ref[...]
    kk = kkr * lax.rsqrt(_segsum(kkr * kkr, ones_bd) + 1e-12)
    ksum = jnp.zeros_like(k)
    for d in range(2):
        a_d = a2[:, d * RC:(d + 1) * RC]
        kd = k * (1.0 + (a_d - 1.0) * ka_ref[...])
        ksum = ksum + kd
        lw_out[d] = lw2[:, d * RC:(d + 1) * RC]
        kd_out[d] = kd
        kka_out[d] = kk * a_d
    r_out[...] = r
    v_out[...] = v
    kk_out[...] = kk
    g_out[...] = g
    bonus_out[...] = _segsum(r * ksum * rk_ref[...], ones_bd) * v


def _blockdiag2(w):
    z = jnp.zeros_like(w[0])
    return jnp.concatenate([jnp.concatenate([w[0], z], 1), jnp.concatenate([z, w[1]], 1)], 0)


def _ones_bd():
    idx = jnp.arange(RC) // RH
    return (idx[:, None] == idx[None, :]).astype(BF16)


def _front(x, tab, v_first, w_in_l, q_norm_g, kv_norm_g, w_uq, w_ukv, p, layer, grp, tm):
    n = grp.n
    has_vmix = layer > 0
    wz, mla_w = _mla_weights(w_in_l, q_norm_g, kv_norm_g, w_uq, w_ukv)
    row2 = lambda a: a.reshape(1, -1)
    mu = row2(p['tshift_mu'][layer])
    wup = _blockdiag2(p['w_lora_up'][layer]).astype(BF16)
    w0 = row2(p['w0'][layer])
    aup = _blockdiag2(p['a_lora_up'][layer]).astype(BF16)
    a0 = row2(p['a0'][layer])
    gup = p['g_lora_up'][layer].astype(BF16)
    kk_p, ka_p, rk_p = row2(p['k_k'][layer]), row2(p['k_a'][layer]), row2(p['r_k'][layer])
    ones_bd = _ones_bd()

    nb8 = n // 8
    bt0, bt1 = grp.t[0] // tm, grp.t[1] // tm
    nb0 = grp.n0 // tm

    def tab_map(i):
        return (jnp.where(i < nb0, lax.rem(i, bt0), lax.rem(i - nb0, bt1)), 0)

    full = lambda a: pl.BlockSpec(a.shape, lambda i: (0,) * a.ndim, pipeline_mode=pl.Buffered(1))
    row = lambda c: pl.BlockSpec((tm, c), lambda i: (i, 0))
    row_d = pl.BlockSpec((2, tm, RC), lambda i: (0, i, 0))
    weights = [wz] + mla_w + [mu, wup, w0, aup, a0, gup, kk_p, ka_p, rk_p, ones_bd]
    in_specs = [row(D_MODEL),
                pl.BlockSpec((8, D_MODEL), lambda i: (jnp.maximum(i * (tm // 8) - 1, 0), 0)),
                pl.BlockSpec((8, D_MODEL), lambda i: (jnp.minimum((i + 1) * (tm // 8), nb8 - 1), 0)),
                pl.BlockSpec((tm, 5 * HPAD), tab_map)] + [full(w) for w in weights]
    args = [x, x, x, tab] + weights
    if has_vmix:
        v0 = row2(p['v0'][layer - 1])
        vd = jnp.pad(p['v_lora_down'][layer - 1], ((0, 0), (0, HPAD - MV_LORA))).astype(BF16)
        vu = jnp.pad(p['v_lora_up'][layer - 1], ((0, HPAD - MV_LORA), (0, 0))).astype(BF16)
        in_specs += [row(RC), full(v0), full(vd), full(vu)]
        args += [v_first, v0, vd, vu]
    qk = jax.ShapeDtypeStruct((n, HEADS * HPAD), BF16)
    one = jax.ShapeDtypeStruct((n, RC), F32)
    two = jax.ShapeDtypeStruct((2, n, RC), F32)
    return pl.pallas_call(
        functools.partial(_front_kernel, grp, tm, has_vmix),
        grid=(n // tm,),
        in_specs=in_specs,
        out_specs=[row(HEADS * HPAD), row(HEADS * HPAD), pl.BlockSpec((HEADS * VDIM, tm), lambda i: (0, i)),
                   row(RC), row(RC), row(RC), row_d, row_d, row_d, row(RC), row(RC)],
        out_shape=[qk, qk, jax.ShapeDtypeStruct((HEADS * VDIM, n), BF16),
                   one, one, one, two, two, two, one, one],
        compiler_params=_cparams(("parallel",)),
        name=f"front_l{layer}",
    )(*args)


def _scan_kernel(nseq, *refs):
    ins = [refs[6 * i:6 * i + 6] for i in range(nseq)]
    y_ref, s_ref = refs[6 * nseq:]
    d = pl.program_id(1)
    c = pl.program_id(2)

    @pl.when(c == 0)
    def _():
        s_ref[...] = jnp.zeros(s_ref.shape, F32)

    sgn = jnp.where(d == 0, 1, -1)
    t_i = lax.broadcasted_iota(jnp.int32, (CHUNK, CHUNK), 0)
    t_j = lax.broadcasted_iota(jnp.int32, (CHUNK, CHUNK), 1)
    cum_mask = jnp.where((t_i - t_j) * sgn >= 0, 1.0, 0.0).astype(BF16)

    bi = lax.broadcasted_iota(jnp.int32, (BD, BD), 0)
    bj = lax.broadcasted_iota(jnp.int32, (BD, BD), 1)
    shift = int(math.log2(CHUNK))
    same_head = (bi >> shift) == (bj >> shift)
    lag = ((bi & (CHUNK - 1)) - (bj & (CHUNK - 1))) * sgn
    strict = same_head & (lag > 0)
    incl = same_head & (lag >= 0)
    eye = jnp.where(bi == bj, 1.0, 0.0)

    def bd(x):
        return jnp.where(same_head, jnp.concatenate([x] * GRP, axis=0), 0.0).astype(BF16)

    pre = []
    for r_ref, v_ref, kk_ref, lw_ref, kd_ref, kka_ref in ins:
        lw = lw_ref[...]
        hi = lw.astype(BF16)
        r1 = lw - hi.astype(F32)
        mid = r1.astype(BF16)
        lo = (r1 - mid.astype(F32)).astype(BF16)
        cs = _dot(cum_mask, hi) + _dot(cum_mask, mid) + _dot(cum_mask, lo)
        gin = jnp.exp(-cs)
        pre.append(dict(a=jnp.exp(cs - lw) * kk_ref[...], b=-kka_ref[...] * gin, k=kd_ref[...] * gin,
                        q=jnp.exp(cs) * r_ref[...], v=v_ref[...],
                        g_last=jnp.exp(jnp.sum(lw, axis=0, keepdims=True))))

    chains = [(i, g) for i in range(nseq) for g in range(HEADS // GRP)]
    st = []
    for i, g in chains:
        cols = slice(g * BD, (g + 1) * BD)
        e = {nm: bd(pre[i][nm][:, cols]) for nm in ('a', 'b', 'k', 'q', 'v')}
        e['g_last'] = pre[i]['g_last'][:, cols]
        st.append(e)
    for e in st:
        prod = _dot_nt(jnp.concatenate([e['a'], e['q']], axis=0), jnp.concatenate([e['b'], e['k']], axis=0))
        e['pw'] = jnp.where(strict, prod[:BD, :BD], 0.0)
        e['tinv'] = eye + e['pw']
        e['l_ak'] = jnp.where(strict, prod[:BD, BD:], 0.0).astype(BF16)
        e['l_qb'] = jnp.where(incl, prod[BD:, :BD], 0.0).astype(BF16)
        e['l_qk'] = jnp.where(incl, prod[BD:, BD:], 0.0).astype(BF16)

    for e in st:
        pwb = e['pw'].astype(BF16)
        e['pw'] = _dot(pwb, pwb)
    for _ in range(int(math.log2(CHUNK)) - 2):
        for e in st:
            pwb = e['pw'].astype(BF16)
            both = _dot(pwb, jnp.concatenate([pwb, e['tinv'].astype(BF16)], axis=1))
            e['pw'] = both[:, :BD]
            e['tinv'] = e['tinv'] + both[:, BD:]
    for e in st:
        e['tinv'] = e['tinv'] + _dot(e['pw'].astype(BF16), e['tinv'].astype(BF16))

    for n, e in enumerate(st):
        e['s0'] = s_ref[n]
        e['s0t'] = e['s0'].T.astype(BF16)
        e['rhs'] = _dot(jnp.concatenate([e['a'], e['l_ak']], axis=1),
                        jnp.concatenate([e['s0t'], e['v']], axis=0))
    for e in st:
        e['ub'] = _dot(e['tinv'].astype(BF16), e['rhs'].astype(BF16)).astype(BF16)
    for (i, g), e in zip(chains, st):
        y = _dot(jnp.concatenate([e['q'], e['l_qb'], e['l_qk']], axis=1),
                 jnp.concatenate([e['s0t'], e['ub'], e['v']], axis=0))
        y_ref[i, :, g * BD:(g + 1) * BD] = sum(y[h * CHUNK:(h + 1) * CHUNK] for h in range(GRP))
    for n, e in enumerate(st):
        upd = _dot_tn(jnp.concatenate([e['ub'], e['v']], axis=0), jnp.concatenate([e['b'], e['k']], axis=0))
        s_ref[n] = (e['s0'] + upd) * e['g_last']


def _rwkv_scan(r, v, kk, lw2, kd2, kka2, grp, gi, nseq):
    b, t = grp.b[gi], grp.t[gi]
    assert b % nseq == 0
    nc = t // CHUNK
    off = grp.off[gi] // CHUNK

    def blk(bi, d, c, i):
        return (bi * nseq + i) * nc + jnp.where(d == 0, c, nc - 1 - c)

    in_specs, args = [], []
    for i in range(nseq):
        one = pl.BlockSpec((CHUNK, RC), lambda bi, d, c, i=i: (off + blk(bi, d, c, i), 0))
        two = pl.BlockSpec((None, CHUNK, RC), lambda bi, d, c, i=i: (d, off + blk(bi, d, c, i), 0))
        in_specs += [one, one, one, two, two, two]
        args += [r, v, kk, lw2, kd2, kka2]
    y = pl.pallas_call(
        functools.partial(_scan_kernel, nseq),
        grid=(b // nseq, 2, nc),
        in_specs=in_specs,
        out_specs=pl.BlockSpec((None, None, nseq, CHUNK, RC),
                               lambda bi, d, c: (d, bi, 0, jnp.where(d == 0, c, nc - 1 - c), 0)),
        out_shape=jax.ShapeDtypeStruct((2, b // nseq, nseq, t, RC), F32),
        scratch_shapes=[pltpu.VMEM((nseq * (HEADS // GRP), BD, BD), F32)],
        compiler_params=_cparams(("parallel", "arbitrary", "arbitrary")),
        name=f"rwkv_scan_g{gi}",
    )(*args)
    return y.reshape(2, b * t, RC)


def _mix_kernel(alpha, nb0, x_ref, att0_ref, att1_ref, y0_ref, y1_ref, g_ref, bonus_ref, lng_ref, lnb_ref,
                ones_ref, wg_ref, wpa_ref, wpb_ref, wo_ref, g1_ref, b1_ref, o_ref):
    in0 = pl.program_id(0) < nb0
    att = jnp.where(in0, att0_ref[...], att1_ref[...])
    ones_bd = ones_ref[...]
    y = jnp.where(in0, y0_ref[0] + y0_ref[1], y1_ref[0] + y1_ref[1])
    mean = _segsum(y, ones_bd, split=True) * (1.0 / RH)
    dlt = y - mean
    var = _segsum(dlt * dlt, ones_bd) * (1.0 / RH)
    yn = dlt * lax.rsqrt(var + GN_EPS) * lng_ref[...] + lnb_ref[...]
    rw = ((yn + bonus_ref[...]) * g_ref[...]).astype(BF16)

    x = x_ref[...]
    gates = _dot(x.astype(BF16), wg_ref[...])
    mixed = (_sigmoid(gates[:, :D_MODEL]) * _dot(att, wpa_ref[...])
             + _sigmoid(gates[:, D_MODEL:]) * _dot(rw, wpb_ref[...]))
    o_ref[...] = _layernorm(alpha * x + _dot(mixed.astype(BF16), wo_ref[...]), g1_ref[...], b1_ref[...])


def _mix(x, att, y2, g, bonus, p, w_in_l, layer, grp, tm, alpha):
    n = grp.n
    nb0, nb1 = grp.n0 // tm, (grp.n - grp.n0) // tm
    idx0 = lambda i: jnp.minimum(i, nb0 - 1)
    idx1 = lambda i: jnp.maximum(i - nb0, 0)
    row2 = lambda a: a.reshape(1, -1)
    lng, lnb = row2(p['lnx_g'][layer]), row2(p['lnx_b'][layer])
    ones_bd = _ones_bd()
    wg = w_in_l[:, OFF_GA:].astype(BF16)
    wpa, wpb, wo = (p[nm][layer].astype(BF16) for nm in ('w_pa', 'w_pb', 'w_o'))
    g1, b1 = row2(p['ln1_g'][layer]), row2(p['ln1_b'][layer])
    full = lambda a: pl.BlockSpec(a.shape, lambda i: (0,) * a.ndim, pipeline_mode=pl.Buffered(1))
    row = lambda c: pl.BlockSpec((tm, c), lambda i: (i, 0))
    return pl.pallas_call(
        functools.partial(_mix_kernel, alpha, nb0),
        grid=(n // tm,),
        in_specs=[row(D_MODEL),
                  pl.BlockSpec((tm, RC), lambda i: (idx0(i), 0)), pl.BlockSpec((tm, RC), lambda i: (idx1(i), 0)),
                  pl.BlockSpec((2, tm, RC), lambda i: (0, idx0(i), 0)),
                  pl.BlockSpec((2, tm, RC), lambda i: (0, idx1(i), 0)),
                  row(RC), row(RC),
                  full(lng), full(lnb), full(ones_bd), full(wg), full(wpa), full(wpb), full(wo),
                  full(g1), full(b1)],
        out_specs=row(D_MODEL),
        out_shape=jax.ShapeDtypeStruct((n, D_MODEL), F32),
        compiler_params=_cparams(("parallel",)),
        name=f"mix_l{layer}",
    )(x, att[0], att[1], y2[0], y2[1], g, bonus, lng, lnb, ones_bd, wg, wpa, wpb, wo, g1, b1)


def _ffn_kernel(grp, tm, alpha, tile0, x_ref, xp_ref, xn_ref, p0_ref, p1_ref, wup_ref, cw_ref, cb_ref, wdn_ref,
                wpg_ref, wpp_ref, g2_ref, b2_ref, o_ref, ue_ref, hg_ref):
    row0 = (pl.program_id(0) + tile0) * tm
    pos, tlen = grp.seq_pos(row0)
    pemb = jnp.where(row0 < grp.n0, p0_ref[...], p1_ref[...])
    keep_prev = (pos != 0).astype(F32)
    keep_next = (pos + tm != tlen).astype(F32)
    x = x_ref[...]
    xb = x.astype(BF16)
    xe = jnp.concatenate([(xp_ref[...] * keep_prev).astype(BF16), xb,
                          (xn_ref[...] * keep_next).astype(BF16)], axis=0)
    acc = alpha * x + _sigmoid(_dot(xb, wpg_ref[...])) * _dot(pemb.astype(BF16), wpp_ref[...])
    ue_ref[0] = _dot(xe, wup_ref[0])
    for j in range(N_FF):
        buf = j % 2
        if j + 1 < N_FF:
            ue_ref[1 - buf] = _dot(xe, wup_ref[j + 1])
        cw = cw_ref[j]
        ue = ue_ref[buf]
        rows = ue.shape[0]
        u_prev = pltpu.roll(ue, 1, 0)[HALO:HALO + tm]
        u_next = pltpu.roll(ue, rows - 1, 0)[HALO:HALO + tm]
        u = cw[0:1] * u_prev + cw[1:2] * ue[HALO:HALO + tm] + cw[2:3] * u_next + cb_ref[j]
        hg = jax.nn.gelu(u[:, :FF_BLK]) * u[:, FF_BLK:]
        hg_ref[:, j * FF_BLK:(j + 1) * FF_BLK] = hg.astype(BF16)
    acc = acc + _dot(hg_ref[...], wdn_ref[...])
    o_ref[...] = _layernorm(acc, g2_ref[...], b2_ref[...])


def _ffn(x, pe, p, layer, grp, tm, alpha, gi=None):
    n = grp.n
    tile0 = 0 if gi is None else grp.off[gi] // tm
    ntiles = n // tm if gi is None else grp.b[gi] * grp.t[gi] // tm
    row2 = lambda a: a.reshape(1, -1)

    def interleave(a):
        lead = a.shape[:-1]
        a = a.reshape(lead + (2, N_FF, FF_BLK))
        a = jnp.moveaxis(a, -2, 0)
        return a.reshape((N_FF,) + lead + (2 * FF_BLK,))

    wup = interleave(p['w_ffn_up'][layer]).astype(BF16)
    cw = jnp.pad(interleave(p['conv_w'][layer]), ((0, 0), (0, 5), (0, 0)))
    cb = interleave(p['conv_b'][layer].reshape(1, -1))
    wdn = p['w_ffn_down'][layer].astype(BF16)
    wpg = p['w_pe_gate'][layer].astype(BF16)
    wpp = p['w_pe_proj'][layer].astype(BF16)
    g2, b2 = row2(p['ln2_g'][layer]), row2(p['ln2_b'][layer])

    nbh = n // HALO
    nb0 = grp.n0 // tm
    pe_spec = lambda idx: pl.BlockSpec((None, tm, P_DIM), lambda i: (layer, idx(i + tile0), 0))
    full = lambda a: pl.BlockSpec(a.shape, lambda i: (0,) * a.ndim, pipeline_mode=pl.Buffered(1))
    hpt = tm // HALO
    return pl.pallas_call(
        functools.partial(_ffn_kernel, grp, tm, alpha, tile0),
        grid=(ntiles,),
        in_specs=[pl.BlockSpec((tm, D_MODEL), lambda i: (i + tile0, 0)),
                  pl.BlockSpec((HALO, D_MODEL), lambda i: (jnp.maximum((i + tile0) * hpt - 1, 0), 0)),
                  pl.BlockSpec((HALO, D_MODEL), lambda i: (jnp.minimum((i + tile0 + 1) * hpt, nbh - 1), 0)),
                  pe_spec(lambda i: jnp.minimum(i, nb0 - 1)), pe_spec(lambda i: jnp.maximum(i - nb0, 0)),
                  full(wup), full(cw), full(cb), full(wdn), full(wpg), full(wpp), full(g2), full(b2)],
        out_specs=pl.BlockSpec((tm, D_MODEL), lambda i: (i, 0)),
        out_shape=jax.ShapeDtypeStruct((ntiles * tm, D_MODEL), F32),
        scratch_shapes=[pltpu.VMEM((2, tm + 2 * HALO, 2 * FF_BLK), F32), pltpu.VMEM((tm, D_FF), BF16)],
        compiler_params=_cparams(("parallel",)),
        name=f"ffn_l{layer}" + ("" if gi is None else f"_g{gi}"),
    )(x, x, x, pe[0], pe[1], wup, cw, cb, wdn, wpg, wpp, g2, b2)


def kernel(x_prompt, x_sample, p_prompt, p_sample, w_in, q_norm_g, kv_norm_g, w_uq, w_ukv, tshift_mu, w0, w_lora_up, a0, a_lora_up, g_lora_up, k_k, k_a, r_k, v0, v_lora_down, v_lora_up, lnx_g, lnx_b, w_pa, w_pb, w_o, ln1_g, ln1_b, w_ffn_up, conv_w, conv_b, w_ffn_down, w_pe_gate, w_pe_proj, ln2_g, ln2_b):
    depth = w_in.shape[0]
    alpha = (2 * depth) ** 0.25
    b0, t0, _ = x_prompt.shape
    b1, t1, _ = x_sample.shape
    grp = _Groups(b0, t0, b1, t1)
    params = dict(tshift_mu=tshift_mu, w0=w0, w_lora_up=w_lora_up, a0=a0, a_lora_up=a_lora_up,
                  g_lora_up=g_lora_up, k_k=k_k, k_a=k_a, r_k=r_k, v0=v0, v_lora_down=v_lora_down,
                  v_lora_up=v_lora_up, lnx_g=lnx_g, lnx_b=lnx_b, w_pa=w_pa, w_pb=w_pb, w_o=w_o,
                  ln1_g=ln1_g, ln1_b=ln1_b, w_ffn_up=w_ffn_up, conv_w=conv_w, conv_b=conv_b,
                  w_ffn_down=w_ffn_down, w_pe_gate=w_pe_gate, w_pe_proj=w_pe_proj, ln2_g=ln2_g,
                  ln2_b=ln2_b)

    tmin = min(t0, t1)
    tm = min(256, tmin)
    tm2 = min(512, tmin)
    tq = min(1024, tmin)
    tk = min(2048, tmin)
    x = jnp.concatenate([x_prompt.reshape(b0 * t0, D_MODEL), x_sample.reshape(b1 * t1, D_MODEL)], 0)
    pe = (p_prompt.reshape(depth, b0 * t0, P_DIM), p_sample.reshape(depth, b1 * t1, P_DIM))
    tab = _rope_tables(max(t0, t1))

    v_first = None
    for layer in range(depth):
        q, k, v, r, vv, kk, lw2, kd2, kka2, g, bonus = _front(
            x, tab, v_first, w_in[layer], q_norm_g[layer], kv_norm_g[layer], w_uq[layer], w_ukv[layer],
            params, layer, grp, tm)
        att = [_attention(q, k, v, grp, gi, tq, tk) for gi in range(2)]
        if layer == 0:
            v_first = vv
        y2 = [_rwkv_scan(r, vv, kk, lw2, kd2, kka2, grp, gi, math.gcd(grp.b[gi], SCAN_SEQS)) for gi in range(2)]
        x = _mix(x, att, y2, g, bonus, params, w_in[layer], layer, grp, tm2, alpha)
        if layer + 1 < depth:
            x = _ffn(x, pe, params, layer, grp, tm2, alpha)

    y0, y1 = (_ffn(x, pe, params, depth - 1, grp, tm2, alpha, gi) for gi in range(2))
    return (y0.reshape(b0, t0, D_MODEL), y1.reshape(b1, t1, D_MODEL))
```

```python
import functools
import math

import jax
import jax.numpy as jnp
from jax import lax
from jax.experimental import pallas as pl
from jax.experimental.pallas import tpu as pltpu

F32 = jnp.float32
BF16 = jnp.bfloat16

D_MODEL = 1024
HEADS = 8
NOPE = 64
ROPE = 32
HALF = ROPE // 2
VDIM = 64
QK = NOPE + ROPE
HPAD = 128
Q_LORA = 768
KV_LORA = 256
ROPE_THETA = 10000.0
RH = 64
RC = HEADS * RH
LORA = 64
GATE_LORA = 128
MV_LORA = 32
DECAY_SCALE = 0.606531
GN_EPS = 64e-5
D_FF = 2816
FF_BLK = 256
N_FF = D_FF // FF_BLK
P_DIM = 256
LN_EPS = 1e-5
RMS_EPS = 1e-6
CHUNK = 64
GRP = 2
BD = GRP * CHUNK
SCAN_SEQS = 8
HALO = 16
ATT_HEADS = 4
ATT_SEGS = 4
ATT_ONES = 16

OFF_CKV = Q_LORA
OFF_KR = OFF_CKV + KV_LORA
OFF_RWKV = OFF_KR + ROPE
RWKV_COLS = 3 * RC + 2 * LORA + 2 * LORA + GATE_LORA
OFF_GA = OFF_RWKV + RWKV_COLS
OFF_GB = OFF_GA + D_MODEL

VMEM_LIMIT = 56 * 1024 * 1024


def _cparams(sem):
    return pltpu.CompilerParams(dimension_semantics=sem, vmem_limit_bytes=VMEM_LIMIT)


def _dot(a, b):
    return jnp.dot(a, b, preferred_element_type=F32)


def _dot_nt(a, b):
    return lax.dot_general(a, b, (((1,), (1,)), ((), ())), preferred_element_type=F32)


def _dot_tn(a, b):
    return lax.dot_general(a, b, (((0,), (0,)), ((), ())), preferred_element_type=F32)


def _split2(x):
    hi = x.astype(BF16)
    lo = (x - hi.astype(F32)).astype(BF16)
    return hi, lo


def _segsum(x, ones_bd, split=False):
    if not split:
        return _dot(x.astype(BF16), ones_bd)
    hi, lo = _split2(x)
    return _dot(hi, ones_bd) + _dot(lo, ones_bd)


def _sigmoid(x):
    return 1.0 / (1.0 + jnp.exp(-x))


def _layernorm(x, g, b):
    mu = jnp.mean(x, axis=-1, keepdims=True)
    d = x - mu
    var = jnp.mean(d * d, axis=-1, keepdims=True)
    return d * lax.rsqrt(var + LN_EPS) * g + b


class _Groups:
    def __init__(self, b0, t0, b1, t1):
        self.b = (b0, b1)
        self.t = (t0, t1)
        self.n0 = b0 * t0
        self.n = b0 * t0 + b1 * t1
        self.off = (0, self.n0)

    def seq_pos(self, row0):
        in0 = row0 < self.n0
        pos = jnp.where(in0, lax.rem(row0, self.t[0]), lax.rem(row0 - self.n0, self.t[1]))
        tlen = jnp.where(in0, self.t[0], self.t[1])
        return pos, tlen


def _mla_part(xb, tab_ref, w1_ref, qg_ref, kvg_ref, wuq_ref, wuk_ref, wuv_ref, e_ref, q_ref, k_ref, v_ref):
    h1 = _dot(xb, w1_ref[...])
    cq = h1[:, :Q_LORA]
    cqn = cq * lax.rsqrt(jnp.mean(cq * cq, axis=-1, keepdims=True) + RMS_EPS) * qg_ref[...]
    q = _dot(cqn.astype(BF16), wuq_ref[...])
    cq_tab = tab_ref[:, 0:HPAD]
    sa_tab = tab_ref[:, HPAD:2 * HPAD]
    sb_tab = tab_ref[:, 2 * HPAD:3 * HPAD]
    for h in range(HEADS):
        qh = q[:, h * HPAD:(h + 1) * HPAD]
        out = qh * cq_tab + pltpu.roll(qh, HALF, 1) * sa_tab + pltpu.roll(qh, HPAD - HALF, 1) * sb_tab
        q_ref[:, h * HPAD:(h + 1) * HPAD] = out.astype(BF16)

    ckv = h1[:, OFF_CKV:OFF_KR]
    ckvn = ckv * lax.rsqrt(jnp.mean(ckv * ckv, axis=-1, keepdims=True) + RMS_EPS) * kvg_ref[...]
    ckvb = ckvn.astype(BF16)
    kr = h1[:, OFF_KR:OFF_KR + HPAD]
    krs = h1[:, OFF_KR + HPAD:OFF_KR + 2 * HPAD]
    krot = kr * tab_ref[:, 3 * HPAD:4 * HPAD] + krs * tab_ref[:, 4 * HPAD:5 * HPAD]
    k = _dot(ckvb, wuk_ref[...]) + _dot(krot.astype(BF16), e_ref[...])
    k_ref[...] = k.astype(BF16)
    v_ref[...] = _dot_nt(wuv_ref[...], ckvb).astype(BF16)


def _rope_tables(t_max):
    pos = jnp.arange(t_max, dtype=F32)
    inv_freq = ROPE_THETA ** (-jnp.arange(0, ROPE, 2, dtype=F32) / ROPE)
    ang = pos[:, None] * inv_freq[None, :]
    cos, sin = jnp.cos(ang), jnp.sin(ang)
    scale = QK ** -0.5 * math.log2(math.e)
    zeros = lambda n: jnp.zeros((t_max, n), F32)
    cq = jnp.concatenate([jnp.ones((t_max, NOPE), F32), cos, cos, zeros(HPAD - QK)], 1) * scale
    sa = jnp.concatenate([zeros(NOPE + HALF), sin, zeros(HPAD - QK)], 1) * scale
    sb = jnp.concatenate([zeros(NOPE), -sin, zeros(HALF + HPAD - QK)], 1) * scale
    ck = jnp.concatenate([cos, cos, zeros(HPAD - ROPE)], 1)
    sk = jnp.concatenate([-sin, sin, zeros(HPAD - ROPE)], 1)
    return jnp.concatenate([cq, sa, sb, ck, sk], 1)


def _mla_weights(w_in_l, q_norm_g, kv_norm_g, w_uq, w_ukv):
    w_kr = w_in_l[:, OFF_KR:OFF_RWKV]
    pad = jnp.zeros((D_MODEL, HPAD - ROPE), F32)
    w1 = jnp.concatenate([w_in_l[:, :OFF_KR], w_kr, pad,
                          w_kr[:, HALF:], w_kr[:, :HALF], pad], 1).astype(BF16)
    wz = w_in_l[:, OFF_RWKV:OFF_GA].astype(BF16)
    wuq = jnp.pad(w_uq.reshape(Q_LORA, HEADS, QK), ((0, 0), (0, 0), (0, HPAD - QK)))
    wuq = wuq.reshape(Q_LORA, HEADS * HPAD).astype(BF16)
    wkv = w_ukv.reshape(KV_LORA, HEADS, NOPE + VDIM)
    wuk = jnp.pad(wkv[:, :, :NOPE], ((0, 0), (0, 0), (0, HPAD - NOPE)))
    wuk = wuk.reshape(KV_LORA, HEADS * HPAD).astype(BF16)
    wuv = wkv[:, :, NOPE:].reshape(KV_LORA, HEADS * VDIM).T.astype(BF16)
    place = (jnp.arange(HEADS * HPAD)[None, :] % HPAD) == (NOPE + jnp.arange(HPAD)[:, None])
    place = jnp.where(jnp.arange(HPAD)[:, None] < ROPE, place, False).astype(BF16)
    qg = q_norm_g.reshape(1, Q_LORA)
    kvg = kv_norm_g.reshape(1, KV_LORA)
    return wz, [w1, qg, kvg, wuq, wuk, wuv, place]


def _attn_kernel(q_ref, k_ref, vt_ref, o_ref, m_ref, acc_ref):
    ki = pl.program_id(3)
    nh = q_ref.shape[1] // HPAD

    @pl.when(ki == 0)
    def _():
        m_ref[...] = jnp.full(m_ref.shape, -jnp.inf, F32)
        acc_ref[...] = jnp.zeros(acc_ref.shape, F32)

    tk, tq = k_ref.shape[0], q_ref.shape[0]
    seg = ATT_SEGS if tk % (8 * ATT_SEGS) == 0 else 1

    def colreduce(fn, x):
        part = fn(x.reshape(seg, tk // (8 * seg), 8, tq), axis=1)
        return fn(fn(part, axis=0), axis=0, keepdims=True)

    def scores(h):
        hs = slice(h * HPAD, (h + 1) * HPAD)
        return _dot_nt(k_ref[:, hs], q_ref[:, hs])

    ones_rows = (lax.broadcasted_iota(jnp.int32, (ATT_ONES, tk), 0) == 0).astype(BF16)
    st_next = scores(0)
    for h in range(nh):
        st = st_next
        if h + 1 < nh:
            st_next = scores(h + 1)
        m_prev = m_ref[h]
        m_new = jnp.maximum(m_prev, colreduce(jnp.max, st))
        alpha = jnp.exp2(m_prev - m_new)
        p = jnp.exp2(st - m_new).astype(BF16)
        m_ref[h] = m_new
        vt1 = jnp.concatenate([vt_ref[h * VDIM:(h + 1) * VDIM, :], ones_rows], axis=0)
        acc_ref[h] = alpha * acc_ref[h] + _dot(vt1, p)

    @pl.when(ki == pl.num_programs(3) - 1)
    def _():
        for h in range(nh):
            acc = acc_ref[h]
            o = acc[:VDIM] / acc[VDIM:VDIM + 1]
            o_ref[:, h * VDIM:(h + 1) * VDIM] = o.T.astype(BF16)


def _attention(q, k, vt, grp, gi, tq, tk):
    b, t = grp.b[gi], grp.t[gi]
    nq, nk = t // tq, t // tk
    qoff, koff = grp.off[gi] // tq, grp.off[gi] // tk
    nh = ATT_HEADS
    qmap = lambda bi, qi, hg, ki: (qoff + bi * nq + qi, hg)
    kmap = lambda bi, qi, hg, ki: (koff + bi * nk + ki, hg)
    return pl.pallas_call(
        _attn_kernel,
        grid=(b, nq, HEADS // nh, nk),
        in_specs=[pl.BlockSpec((tq, nh * HPAD), qmap), pl.BlockSpec((tk, nh * HPAD), kmap),
                  pl.BlockSpec((nh * VDIM, tk), lambda bi, qi, hg, ki: (hg, koff + bi * nk + ki))],
        out_specs=pl.BlockSpec((tq, nh * VDIM), lambda bi, qi, hg, ki: (bi * nq + qi, hg)),
        out_shape=jax.ShapeDtypeStruct((b * t, HEADS * VDIM), BF16),
        scratch_shapes=[pltpu.VMEM((nh, 1, tq), F32), pltpu.VMEM((nh, VDIM + ATT_ONES, tq), F32)],
        compiler_params=_cparams(("parallel", "parallel", "parallel", "arbitrary")),
        name=f"attention_g{gi}",
    )(q, k, vt)


def _front_kernel(grp, tm, has_vmix, *refs):
    (x_ref, xp_ref, xn_ref, tab_ref, wz_ref, w1_ref, qg_ref, kvg_ref, wuq_ref, wuk_ref, wuv_ref, e_ref,
     mu_ref, wup_ref, w0_ref, aup_ref, a0_ref, gup_ref, kk_ref, ka_ref, rk_ref, ones_ref) = refs[:22]
    if has_vmix:
        vf_ref, v0_ref, vd_ref, vu_ref = refs[22:26]
    (q_ref, k_ref, v_ref,
     r_out, v_out, kk_out, lw_out, kd_out, kka_out, g_out, bonus_out) = refs[-11:]
    xb = x_ref[...].astype(BF16)
    zt = _dot(xb, wz_ref[...])
    halo = jnp.concatenate([xp_ref[...], xn_ref[...]], axis=0).astype(BF16)
    zh = _dot(halo, wz_ref[...])
    _mla_part(xb, tab_ref, w1_ref, qg_ref, kvg_ref, wuq_ref, wuk_ref, wuv_ref, e_ref, q_ref, k_ref, v_ref)

    pos, tlen = grp.seq_pos(pl.program_id(0) * tm)
    first = pos == 0
    last = pos + tm == tlen
    prev_row = jnp.where(first, 0.0, zh[7:8, :])
    next_row = jnp.where(last, 0.0, zh[8:9, :])
    rows = lax.broadcasted_iota(jnp.int32, zt.shape, 0)
    z_prev = jnp.where(rows == 0, prev_row, pltpu.roll(zt, 1, 0))
    z_next = jnp.where(rows == tm - 1, next_row, pltpu.roll(zt, tm - 1, 0))
    zm = zt + mu_ref[...] * (0.5 * (z_prev + z_next) - zt)

    r = zm[:, 0:RC]
    k = zm[:, RC:2 * RC]
    v = zm[:, 2 * RC:3 * RC]
    o = 3 * RC
    zw = zm[:, o:o + 2 * LORA]
    za = zm[:, o + 2 * LORA:o + 4 * LORA]
    zg = zm[:, o + 4 * LORA:o + 4 * LORA + GATE_LORA]

    lw2 = -DECAY_SCALE * _sigmoid(w0_ref[...] + _dot(jnp.tanh(zw).astype(BF16), wup_ref[...]))
    a2 = _sigmoid(a0_ref[...] + _dot(za.astype(BF16), aup_ref[...]))
    g = _dot(_sigmoid(zg).astype(BF16), gup_ref[...])
    if has_vmix:
        xd = _dot(xb, vd_ref[...])
        gate = _sigmoid(v0_ref[...] + _dot(xd.astype(BF16), vu_ref[...]))
        v = v + (vf_ref[...] - v) * gate

    ones_bd = ones_ref[...]
    kkr = k * kk_ref[...]
    kk = kkr * lax.rsqrt(_segsum(kkr * kkr, ones_bd) + 1e-12)
    ksum = jnp.zeros_like(k)
    for d in range(2):
        a_d = a2[:, d * RC:(d + 1) * RC]
        kd = k * (1.0 + (a_d - 1.0) * ka_ref[...])
        ksum = ksum + kd
        lw_out[d] = lw2[:, d * RC:(d + 1) * RC]
        kd_out[d] = kd
        kka_out[d] = kk * a_d
    r_out[...] = r
    v_out[...] = v
    kk_out[...] = kk
    g_out[...] = g
    bonus_out[...] = _segsum(r * ksum * rk_ref[...], ones_bd) * v


def _blockdiag2(w):
    z = jnp.zeros_like(w[0])
    return jnp.concatenate([jnp.concatenate([w[0], z], 1), jnp.concatenate([z, w[1]], 1)], 0)


def _ones_bd():
    idx = jnp.arange(RC) // RH
    return (idx[:, None] == idx[None, :]).astype(BF16)


def _front(x, tab, v_first, w_in_l, q_norm_g, kv_norm_g, w_uq, w_ukv, p, layer, grp, tm):
    n = grp.n
    has_vmix = layer > 0
    wz, mla_w = _mla_weights(w_in_l, q_norm_g, kv_norm_g, w_uq, w_ukv)
    row2 = lambda a: a.reshape(1, -1)
    mu = row2(p['tshift_mu'][layer])
    wup = _blockdiag2(p['w_lora_up'][layer]).astype(BF16)
    w0 = row2(p['w0'][layer])
    aup = _blockdiag2(p['a_lora_up'][layer]).astype(BF16)
    a0 = row2(p['a0'][layer])
    gup = p['g_lora_up'][layer].astype(BF16)
    kk_p, ka_p, rk_p = row2(p['k_k'][layer]), row2(p['k_a'][layer]), row2(p['r_k'][layer])
    ones_bd = _ones_bd()

    nb8 = n // 8
    bt0, bt1 = grp.t[0] // tm, grp.t[1] // tm
    nb0 = grp.n0 // tm

    def tab_map(i):
        return (jnp.where(i < nb0, lax.rem(i, bt0), lax.rem(i - nb0, bt1)), 0)

    full = lambda a: pl.BlockSpec(a.shape, lambda i: (0,) * a.ndim, pipeline_mode=pl.Buffered(1))
    row = lambda c: pl.BlockSpec((tm, c), lambda i: (i, 0))
    row_d = pl.BlockSpec((2, tm, RC), lambda i: (0, i, 0))
    weights = [wz] + mla_w + [mu, wup, w0, aup, a0, gup, kk_p, ka_p, rk_p, ones_bd]
    in_specs = [row(D_MODEL),
                pl.BlockSpec((8, D_MODEL), lambda i: (jnp.maximum(i * (tm // 8) - 1, 0), 0)),
                pl.BlockSpec((8, D_MODEL), lambda i: (jnp.minimum((i + 1) * (tm // 8), nb8 - 1), 0)),
                pl.BlockSpec((tm, 5 * HPAD), tab_map)] + [full(w) for w in weights]
    args = [x, x, x, tab] + weights
    if has_vmix:
        v0 = row2(p['v0'][layer - 1])
        vd = jnp.pad(p['v_lora_down'][layer - 1], ((0, 0), (0, HPAD - MV_LORA))).astype(BF16)
        vu = jnp.pad(p['v_lora_up'][layer - 1], ((0, HPAD - MV_LORA), (0, 0))).astype(BF16)
        in_specs += [row(RC), full(v0), full(vd), full(vu)]
        args += [v_first, v0, vd, vu]
    qk = jax.ShapeDtypeStruct((n, HEADS * HPAD), BF16)
    one = jax.ShapeDtypeStruct((n, RC), F32)
    two = jax.ShapeDtypeStruct((2, n, RC), F32)
    return pl.pallas_call(
        functools.partial(_front_kernel, grp, tm, has_vmix),
        grid=(n // tm,),
        in_specs=in_specs,
        out_specs=[row(HEADS * HPAD), row(HEADS * HPAD), pl.BlockSpec((HEADS * VDIM, tm), lambda i: (0, i)),
                   row(RC), row(RC), row(RC), row_d, row_d, row_d, row(RC), row(RC)],
        out_shape=[qk, qk, jax.ShapeDtypeStruct((HEADS * VDIM, n), BF16),
                   one, one, one, two, two, two, one, one],
        compiler_params=_cparams(("parallel",)),
        name=f"front_l{layer}",
    )(*args)


def _scan_kernel(nseq, *refs):
    ins = [refs[6 * i:6 * i + 6] for i in range(nseq)]
    y_ref, s_ref = refs[6 * nseq:]
    d = pl.program_id(1)
    c = pl.program_id(2)

    @pl.when(c == 0)
    def _():
        s_ref[...] = jnp.zeros(s_ref.shape, F32)

    sgn = jnp.where(d == 0, 1, -1)
    t_i = lax.broadcasted_iota(jnp.int32, (CHUNK, CHUNK), 0)
    t_j = lax.broadcasted_iota(jnp.int32, (CHUNK, CHUNK), 1)
    cum_mask = jnp.where((t_i - t_j) * sgn >= 0, 1.0, 0.0).astype(BF16)

    bi = lax.broadcasted_iota(jnp.int32, (BD, BD), 0)
    bj = lax.broadcasted_iota(jnp.int32, (BD, BD), 1)
    shift = int(math.log2(CHUNK))
    same_head = (bi >> shift) == (bj >> shift)
    lag = ((bi & (CHUNK - 1)) - (bj & (CHUNK - 1))) * sgn
    strict = same_head & (lag > 0)
    incl = same_head & (lag >= 0)
    eye = jnp.where(bi == bj, 1.0, 0.0)

    def bd(x):
        return jnp.where(same_head, jnp.concatenate([x] * GRP, axis=0), 0.0).astype(BF16)

    pre = []
    for r_ref, v_ref, kk_ref, lw_ref, kd_ref, kka_ref in ins:
        lw = lw_ref[...]
        hi = lw.astype(BF16)
        r1 = lw - hi.astype(F32)
        mid = r1.astype(BF16)
        lo = (r1 - mid.astype(F32)).astype(BF16)
        cs = _dot(cum_mask, hi) + _dot(cum_mask, mid) + _dot(cum_mask, lo)
        gin = jnp.exp(-cs)
        pre.append(dict(a=jnp.exp(cs - lw) * kk_ref[...], b=-kka_ref[...] * gin, k=kd_ref[...] * gin,
                        q=jnp.exp(cs) * r_ref[...], v=v_ref[...],
                        g_last=jnp.exp(jnp.sum(lw, axis=0, keepdims=True))))

    chains = [(i, g) for i in range(nseq) for g in range(HEADS // GRP)]
    st = []
    for i, g in chains:
        cols = slice(g * BD, (g + 1) * BD)
        e = {nm: bd(pre[i][nm][:, cols]) for nm in ('a', 'b', 'k', 'q', 'v')}
        e['g_last'] = pre[i]['g_last'][:, cols]
        st.append(e)
    for e in st:
        prod = _dot_nt(jnp.concatenate([e['a'], e['q']], axis=0), jnp.concatenate([e['b'], e['k']], axis=0))
        e['pw'] = jnp.where(strict, prod[:BD, :BD], 0.0)
        e['tinv'] = eye + e['pw']
        e['l_ak'] = jnp.where(strict, prod[:BD, BD:], 0.0).astype(BF16)
        e['l_qb'] = jnp.where(incl, prod[BD:, :BD], 0.0).astype(BF16)
        e['l_qk'] = jnp.where(incl, prod[BD:, BD:], 0.0).astype(BF16)

    for e in st:
        pwb = e['pw'].astype(BF16)
        e['pw'] = _dot(pwb, pwb)
    for _ in range(int(math.log2(CHUNK)) - 2):
        for e in st:
            pwb = e['pw'].astype(BF16)
            both = _dot(pwb, jnp.concatenate([pwb, e['tinv'].astype(BF16)], axis=1))
            e['pw'] = both[:, :BD]
            e['tinv'] = e['tinv'] + both[:, BD:]
    for e in st:
        e['tinv'] = e['tinv'] + _dot(e['pw'].astype(BF16), e['tinv'].astype(BF16))

    for n, e in enumerate(st):
        e['s0'] = s_ref[n]
        e['s0t'] = e['s0'].T.astype(BF16)
        e['rhs'] = _dot(jnp.concatenate([e['a'], e['l_ak']], axis=1),
                        jnp.concatenate([e['s0t'], e['v']], axis=0))
    for e in st:
        e['ub'] = _dot(e['tinv'].astype(BF16), e['rhs'].astype(BF16)).astype(BF16)
    for (i, g), e in zip(chains, st):
        y = _dot(jnp.concatenate([e['q'], e['l_qb'], e['l_qk']], axis=1),
                 jnp.concatenate([e['s0t'], e['ub'], e['v']], axis=0))
        y_ref[i, :, g * BD:(g + 1) * BD] = sum(y[h * CHUNK:(h + 1) * CHUNK] for h in range(GRP))
    for n, e in enumerate(st):
        upd = _dot_tn(jnp.concatenate([e['ub'], e['v']], axis=0), jnp.concatenate([e['b'], e['k']], axis=0))
        s_ref[n] = (e['s0'] + upd) * e['g_last']


def _rwkv_scan(r, v, kk, lw2, kd2, kka2, grp, gi, nseq):
    b, t = grp.b[gi], grp.t[gi]
    assert b % nseq == 0
    nc = t // CHUNK
    off = grp.off[gi] // CHUNK

    def blk(bi, d, c, i):
        return (bi * nseq + i) * nc + jnp.where(d == 0, c, nc - 1 - c)

    in_specs, args = [], []
    for i in range(nseq):
        one = pl.BlockSpec((CHUNK, RC), lambda bi, d, c, i=i: (off + blk(bi, d, c, i), 0))
        two = pl.BlockSpec((None, CHUNK, RC), lambda bi, d, c, i=i: (d, off + blk(bi, d, c, i), 0))
        in_specs += [one, one, one, two, two, two]
        args += [r, v, kk, lw2, kd2, kka2]
    y = pl.pallas_call(
        functools.partial(_scan_kernel, nseq),
        grid=(b // nseq, 2, nc),
        in_specs=in_specs,
        out_specs=pl.BlockSpec((None, None, nseq, CHUNK, RC),
                               lambda bi, d, c: (d, bi, 0, jnp.where(d == 0, c, nc - 1 - c), 0)),
        out_shape=jax.ShapeDtypeStruct((2, b // nseq, nseq, t, RC), F32),
        scratch_shapes=[pltpu.VMEM((nseq * (HEADS // GRP), BD, BD), F32)],
        compiler_params=_cparams(("parallel", "arbitrary", "arbitrary")),
        name=f"rwkv_scan_g{gi}",
    )(*args)
    return y.reshape(2, b * t, RC)


def _mix_kernel(alpha, nb0, x_ref, att0_ref, att1_ref, y0_ref, y1_ref, g_ref, bonus_ref, lng_ref, lnb_ref,
                ones_ref, wg_ref, wpa_ref, wpb_ref, wo_ref, g1_ref, b1_ref, o_ref):
    in0 = pl.program_id(0) < nb0
    att = jnp.where(in0, att0_ref[...], att1_ref[...])
    ones_bd = ones_ref[...]
    y = jnp.where(in0, y0_ref[0] + y0_ref[1], y1_ref[0] + y1_ref[1])
    mean = _segsum(y, ones_bd, split=True) * (1.0 / RH)
    dlt = y - mean
    var = _segsum(dlt * dlt, ones_bd) * (1.0 / RH)
    yn = dlt * lax.rsqrt(var + GN_EPS) * lng_ref[...] + lnb_ref[...]
    rw = ((yn + bonus_ref[...]) * g_ref[...]).astype(BF16)

    x = x_ref[...]
    gates = _dot(x.astype(BF16), wg_ref[...])
    mixed = (_sigmoid(gates[:, :D_MODEL]) * _dot(att, wpa_ref[...])
             + _sigmoid(gates[:, D_MODEL:]) * _dot(rw, wpb_ref[...]))
    o_ref[...] = _layernorm(alpha * x + _dot(mixed.astype(BF16), wo_ref[...]), g1_ref[...], b1_ref[...])


def _mix(x, att, y2, g, bonus, p, w_in_l, layer, grp, tm, alpha):
    n = grp.n
    nb0, nb1 = grp.n0 // tm, (grp.n - grp.n0) // tm
    idx0 = lambda i: jnp.minimum(i, nb0 - 1)
    idx1 = lambda i: jnp.maximum(i - nb0, 0)
    row2 = lambda a: a.reshape(1, -1)
    lng, lnb = row2(p['lnx_g'][layer]), row2(p['lnx_b'][layer])
    ones_bd = _ones_bd()
    wg = w_in_l[:, OFF_GA:].astype(BF16)
    wpa, wpb, wo = (p[nm][layer].astype(BF16) for nm in ('w_pa', 'w_pb', 'w_o'))
    g1, b1 = row2(p['ln1_g'][layer]), row2(p['ln1_b'][layer])
    full = lambda a: pl.BlockSpec(a.shape, lambda i: (0,) * a.ndim, pipeline_mode=pl.Buffered(1))
    row = lambda c: pl.BlockSpec((tm, c), lambda i: (i, 0))
    return pl.pallas_call(
        functools.partial(_mix_kernel, alpha, nb0),
        grid=(n // tm,),
        in_specs=[row(D_MODEL),
                  pl.BlockSpec((tm, RC), lambda i: (idx0(i), 0)), pl.BlockSpec((tm, RC), lambda i: (idx1(i), 0)),
                  pl.BlockSpec((2, tm, RC), lambda i: (0, idx0(i), 0)),
                  pl.BlockSpec((2, tm, RC), lambda i: (0, idx1(i), 0)),
                  row(RC), row(RC),
                  full(lng), full(lnb), full(ones_bd), full(wg), full(wpa), full(wpb), full(wo),
                  full(g1), full(b1)],
        out_specs=row(D_MODEL),
        out_shape=jax.ShapeDtypeStruct((n, D_MODEL), F32),
        compiler_params=_cparams(("parallel",)),
        name=f"mix_l{layer}",
    )(x, att[0], att[1], y2[0], y2[1], g, bonus, lng, lnb, ones_bd, wg, wpa, wpb, wo, g1, b1)


def _ffn_kernel(grp, tm, alpha, tile0, x_ref, xp_ref, xn_ref, p0_ref, p1_ref, wup_ref, cw_ref, cb_ref, wdn_ref,
                wpg_ref, wpp_ref, g2_ref, b2_ref, o_ref, ue_ref, hg_ref):
    row0 = (pl.program_id(0) + tile0) * tm
    pos, tlen = grp.seq_pos(row0)
    pemb = jnp.where(row0 < grp.n0, p0_ref[...], p1_ref[...])
    keep_prev = (pos != 0).astype(F32)
    keep_next = (pos + tm != tlen).astype(F32)
    x = x_ref[...]
    xb = x.astype(BF16)
    xe = jnp.concatenate([(xp_ref[...] * keep_prev).astype(BF16), xb,
                          (xn_ref[...] * keep_next).astype(BF16)], axis=0)
    acc = alpha * x + _sigmoid(_dot(xb, wpg_ref[...])) * _dot(pemb.astype(BF16), wpp_ref[...])
    ue_ref[0] = _dot(xe, wup_ref[0])
    for j in range(N_FF):
        buf = j % 2
        if j + 1 < N_FF:
            ue_ref[1 - buf] = _dot(xe, wup_ref[j + 1])
        cw = cw_ref[j]
        ue = ue_ref[buf]
        rows = ue.shape[0]
        u_prev = pltpu.roll(ue, 1, 0)[HALO:HALO + tm]
        u_next = pltpu.roll(ue, rows - 1, 0)[HALO:HALO + tm]
        u = cw[0:1] * u_prev + cw[1:2] * ue[HALO:HALO + tm] + cw[2:3] * u_next + cb_ref[j]
        hg = jax.nn.gelu(u[:, :FF_BLK]) * u[:, FF_BLK:]
        hg_ref[:, j * FF_BLK:(j + 1) * FF_BLK] = hg.astype(BF16)
    acc = acc + _dot(hg_ref[...], wdn_ref[...])
    o_ref[...] = _layernorm(acc, g2_ref[...], b2_ref[...])


def _ffn(x, pe, p, layer, grp, tm, alpha, gi=None):
    n = grp.n
    tile0 = 0 if gi is None else grp.off[gi] // tm
    ntiles = n // tm if gi is None else grp.b[gi] * grp.t[gi] // tm
    row2 = lambda a: a.reshape(1, -1)

    def interleave(a):
        lead = a.shape[:-1]
        a = a.reshape(lead + (2, N_FF, FF_BLK))
        a = jnp.moveaxis(a, -2, 0)
        return a.reshape((N_FF,) + lead + (2 * FF_BLK,))

    wup = interleave(p['w_ffn_up'][layer]).astype(BF16)
    cw = jnp.pad(interleave(p['conv_w'][layer]), ((0, 0), (0, 5), (0, 0)))
    cb = interleave(p['conv_b'][layer].reshape(1, -1))
    wdn = p['w_ffn_down'][layer].astype(BF16)
    wpg = p['w_pe_gate'][layer].astype(BF16)
    wpp = p['w_pe_proj'][layer].astype(BF16)
    g2, b2 = row2(p['ln2_g'][layer]), row2(p['ln2_b'][layer])

    nbh = n // HALO
    nb0 = grp.n0 // tm
    pe_spec = lambda idx: pl.BlockSpec((None, tm, P_DIM), lambda i: (layer, idx(i + tile0), 0))
    full = lambda a: pl.BlockSpec(a.shape, lambda i: (0,) * a.ndim, pipeline_mode=pl.Buffered(1))
    hpt = tm // HALO
    return pl.pallas_call(
        functools.partial(_ffn_kernel, grp, tm, alpha, tile0),
        grid=(ntiles,),
        in_specs=[pl.BlockSpec((tm, D_MODEL), lambda i: (i + tile0, 0)),
                  pl.BlockSpec((HALO, D_MODEL), lambda i: (jnp.maximum((i + tile0) * hpt - 1, 0), 0)),
                  pl.BlockSpec((HALO, D_MODEL), lambda i: (jnp.minimum((i + tile0 + 1) * hpt, nbh - 1), 0)),
                  pe_spec(lambda i: jnp.minimum(i, nb0 - 1)), pe_spec(lambda i: jnp.maximum(i - nb0, 0)),
                  full(wup), full(cw), full(cb), full(wdn), full(wpg), full(wpp), full(g2), full(b2)],
        out_specs=pl.BlockSpec((tm, D_MODEL), lambda i: (i, 0)),
        out_shape=jax.ShapeDtypeStruct((ntiles * tm, D_MODEL), F32),
        scratch_shapes=[pltpu.VMEM((2, tm + 2 * HALO, 2 * FF_BLK), F32), pltpu.VMEM((tm, D_FF), BF16)],
        compiler_params=_cparams(("parallel",)),
        name=f"ffn_l{layer}" + ("" if gi is None else f"_g{gi}"),
    )(x, x, x, pe[0], pe[1], wup, cw, cb, wdn, wpg, wpp, g2, b2)


def kernel(x_prompt, x_sample, p_prompt, p_sample, w_in, q_norm_g, kv_norm_g, w_uq, w_ukv, tshift_mu, w0, w_lora_up, a0, a_lora_up, g_lora_up, k_k, k_a, r_k, v0, v_lora_down, v_lora_up, lnx_g, lnx_b, w_pa, w_pb, w_o, ln1_g, ln1_b, w_ffn_up, conv_w, conv_b, w_ffn_down, w_pe_gate, w_pe_proj, ln2_g, ln2_b):
    depth = w_in.shape[0]
    alpha = (2 * depth) ** 0.25
    b0, t0, _ = x_prompt.shape
    b1, t1, _ = x_sample.shape
    grp = _Groups(b0, t0, b1, t1)
    params = dict(tshift_mu=tshift_mu, w0=w0, w_lora_up=w_lora_up, a0=a0, a_lora_up=a_lora_up,
                  g_lora_up=g_lora_up, k_k=k_k, k_a=k_a, r_k=r_k, v0=v0, v_lora_down=v_lora_down,
                  v_lora_up=v_lora_up, lnx_g=lnx_g, lnx_b=lnx_b, w_pa=w_pa, w_pb=w_pb, w_o=w_o,
                  ln1_g=ln1_g, ln1_b=ln1_b, w_ffn_up=w_ffn_up, conv_w=conv_w, conv_b=conv_b,
                  w_ffn_down=w_ffn_down, w_pe_gate=w_pe_gate, w_pe_proj=w_pe_proj, ln2_g=ln2_g,
                  ln2_b=ln2_b)

    tmin = min(t0, t1)
    tm = min(256, tmin)
    tm2 = min(512, tmin)
    tq = min(1024, tmin)
    tk = min(2048, tmin)
    x = jnp.concatenate([x_prompt.reshape(b0 * t0, D_MODEL), x_sample.reshape(b1 * t1, D_MODEL)], 0)
    pe = (p_prompt.reshape(depth, b0 * t0, P_DIM), p_sample.reshape(depth, b1 * t1, P_DIM))
    tab = _rope_tables(max(t0, t1))

    v_first = None
    for layer in range(depth):
        q, k, v, r, vv, kk, lw2, kd2, kka2, g, bonus = _front(
            x, tab, v_first, w_in[layer], q_norm_g[layer], kv_norm_g[layer], w_uq[layer], w_ukv[layer],
            params, layer, grp, tm)
        att = [_attention(q, k, v, grp, gi, tq, tk) for gi in range(2)]
        if layer == 0:
            v_first = vv
        y2 = [_rwkv_scan(r, vv, kk, lw2, kd2, kka2, grp, gi, math.gcd(grp.b[gi], SCAN_SEQS)) for gi in range(2)]
        x = _mix(x, att, y2, g, bonus, params, w_in[layer], layer, grp, tm2, alpha)
        if layer + 1 < depth:
            x = _ffn(x, pe, params, layer, grp, tm2, alpha)

    y0, y1 = (_ffn(x, pe, params, depth - 1, grp, tm2, alpha, gi) for gi in range(2))
    return (y0.reshape(b0, t0, D_MODEL), y1.reshape(b1, t1, D_MODEL))
```
